```python
import math
import jax
import jax.numpy as jnp
from jax import lax
import numpy as np

D_MODEL = 1024
BATCH = 8
SEQ = 2048
DEPTH = 4

N_MIXERS = 4
PLE_DIM = 256
ALPHA = (2 * DEPTH) ** 0.25
BETA = (8 * DEPTH) ** -0.25
LN_EPS = 1e-5
RMS_EPS = 1e-6

FOX_HEADS = 16
FOX_HEAD_DIM = D_MODEL // FOX_HEADS
Q_BLOCK = 128

LRU_WIDTH = 1280
LRU_BLOCKS = 10
LRU_BLOCK_DIM = LRU_WIDTH // LRU_BLOCKS
LRU_CONV = 4
LRU_C = 8.0

CONF_KERNEL = 31

GDN_HEADS = 8
GDN_HEAD_DIM = D_MODEL // GDN_HEADS
GDN_CONV = 4
GDN_CHUNK = 64

D_FF = 2816
N_EXPERTS = 8
TOP_K = 2
D_EXPERT = 3584
MOE_BLOCK = 256

N_FOX = len(range(0, DEPTH, N_MIXERS))
N_LRU = len(range(1, DEPTH, N_MIXERS))
N_CONV = len(range(2, DEPTH, N_MIXERS))
N_GDN = len(range(3, DEPTH, N_MIXERS))
N_DENSE = len(range(0, DEPTH, 2))
N_MOE = len(range(1, DEPTH, 2))

kernel_name = "hybrid_fox_rglru_conformer_gdn_moe_trunk"


def _layer_norm(x, g, b):
    xf = x.astype(jnp.float32)
    mu = jnp.mean(xf, axis=-1, keepdims=True)
    var = jnp.mean(jnp.square(xf - mu), axis=-1, keepdims=True)
    return ((xf - mu) * lax.rsqrt(var + LN_EPS) * g + b).astype(x.dtype)


def _rms_norm(x, g):
    xf = x.astype(jnp.float32)
    return xf * lax.rsqrt(jnp.mean(jnp.square(xf), axis=-1, keepdims=True) + RMS_EPS) * g


def _l2_normalize(x):
    return x * lax.rsqrt(jnp.sum(jnp.square(x), axis=-1, keepdims=True) + 1e-6)


def _causal_depthwise_conv(x, w):
    k = w.shape[0]
    xp = jnp.pad(x, ((0, 0), (k - 1, 0), (0, 0)))
    return lax.conv_general_dilated(xp, w[:, None, :].astype(x.dtype), (1,), "VALID",
                                    dimension_numbers=("NWC", "WIO", "NWC"),
                                    feature_group_count=x.shape[-1])


def _swiglu(h, w_gu, w_down):
    gate, up = jnp.split(h @ w_gu, 2, axis=-1)
    return (jax.nn.silu(gate) * up) @ w_down


def _fox_attention(x, w_in, b_f, w_out):
    B, S, _ = x.shape
    H, Dh = FOX_HEADS, FOX_HEAD_DIM
    q, k, v, f_logit = jnp.split(x @ w_in, [D_MODEL, 2 * D_MODEL, 3 * D_MODEL], axis=-1)
    to_heads = lambda t: t.reshape(B, S, H, Dh).transpose(0, 2, 1, 3)
    q, k, v = to_heads(q), to_heads(k), to_heads(v)
    log_f = jax.nn.log_sigmoid(f_logit.astype(jnp.float32) + b_f.astype(jnp.float32))
    c = jnp.cumsum(log_f, axis=1).transpose(0, 2, 1)
    scale = Dh ** -0.5
    outs = []
    for start in range(0, S, Q_BLOCK):
        end = start + Q_BLOCK
        s = jnp.einsum("bhqd,bhkd->bhqk", q[:, :, start:end], k[:, :, :end]).astype(jnp.float32) * scale
        s = s + c[:, :, start:end, None] - c[:, :, None, :end]
        causal = (start + jnp.arange(Q_BLOCK))[:, None] >= jnp.arange(end)[None, :]
        s = jnp.where(causal, s, -jnp.inf)
        pr = jax.nn.softmax(s, axis=-1).astype(v.dtype)
        outs.append(jnp.einsum("bhqk,bhkd->bhqd", pr, v[:, :, :end]))
    o = jnp.concatenate(outs, axis=2).transpose(0, 2, 1, 3).reshape(B, S, H * Dh)
    return o @ w_out


def _linear_recurrence_combine(left, right):
    a_l, b_l = left
    a_r, b_r = right
    return a_l * a_r, a_r * b_l + b_r


def _rglru_block(x, w_in, conv_w, conv_b, w_a, b_a, w_x, b_x, lam, w_out):
    B, S, _ = x.shape
    gate_in, rec_in = jnp.split(x @ w_in, 2, axis=-1)
    u = _causal_depthwise_conv(rec_in, conv_w) + conv_b
    ub = u.reshape(B, S, LRU_BLOCKS, LRU_BLOCK_DIM)
    r = jax.nn.sigmoid(jnp.einsum("bsni,nij->bsnj", ub, w_a).reshape(B, S, LRU_WIDTH) + b_a)
    i = jax.nn.sigmoid(jnp.einsum("bsni,nij->bsnj", ub, w_x).reshape(B, S, LRU_WIDTH) + b_x)
    log_a = -LRU_C * r.astype(jnp.float32) * jax.nn.softplus(-lam.astype(jnp.float32))
    a = jnp.exp(log_a)
    b = jnp.sqrt(-jnp.expm1(2.0 * log_a)) * (i * u).astype(jnp.float32)
    _, h = lax.associative_scan(_linear_recurrence_combine, (a, b), axis=1)
    y = jax.nn.gelu(gate_in) * h.astype(x.dtype)
    return y @ w_out


def _conformer_conv(x, w_in, b_in, dw_w, dw_b, ln_g, ln_b, w_out, b_out):
    val, gate = jnp.split(x @ w_in + b_in, 2, axis=-1)
    h = val * jax.nn.sigmoid(gate)
    h = _causal_depthwise_conv(h, dw_w) + dw_b
    h = jax.nn.silu(_layer_norm(h, ln_g, ln_b))
    return h @ w_out + b_out


def _chunk_gated_delta_rule(q, k, v, g, beta):
    B, H, S, Dk = q.shape
    Dv = v.shape[-1]
    C = GDN_CHUNK
    N = S // C
    rs = lambda t: t.reshape(B, H, N, C, *t.shape[3:])
    q, k, v, g, beta = rs(q), rs(k), rs(v), rs(g), rs(beta)
    g = jnp.cumsum(g, axis=-1)
    k_beta = k * beta[..., None]
    v_beta = v * beta[..., None]
    tril = jnp.tril(jnp.ones((C, C), dtype=bool))
    strict = jnp.tril(jnp.ones((C, C), dtype=bool), -1)
    diff = g[..., :, None] - g[..., None, :]
    decay = jnp.where(tril, jnp.exp(jnp.where(tril, diff, 0.0)), 0.0)
    lower = jnp.where(strict, jnp.einsum("bhncd,bhnsd->bhncs", k_beta, k) * decay, 0.0)
    eye = jnp.eye(C, dtype=jnp.float32)
    rhs = jnp.concatenate([v_beta, k_beta * jnp.exp(g)[..., None]], axis=-1)
    solved = lax.linalg.triangular_solve(eye + lower, rhs, left_side=True, lower=True, unit_diagonal=True)
    u, w = jnp.split(solved, [Dv], axis=-1)
    attn_intra = jnp.where(tril, jnp.einsum("bhncd,bhnsd->bhncs", q, k) * decay, 0.0)
    g_last = g[..., -1]
    k_to_end = k * jnp.exp(g_last[..., None] - g)[..., None]

    def step(state, inp):
        q_c, u_c, w_c, attn_c, g_c, gl_c, kend_c = inp
        v_new = u_c - jnp.einsum("bhcd,bhdv->bhcv", w_c, state)
        o_c = (jnp.einsum("bhcd,bhdv->bhcv", q_c * jnp.exp(g_c)[..., None], state)
               + jnp.einsum("bhcs,bhsv->bhcv", attn_c, v_new))
        state = state * jnp.exp(gl_c)[..., None, None] + jnp.einsum("bhcd,bhcv->bhdv", kend_c, v_new)
        return state, o_c

    xs = tuple(jnp.moveaxis(t, 2, 0) for t in (q, u, w, attn_intra, g, g_last, k_to_end))
    state0 = jnp.zeros((B, H, Dk, Dv), jnp.float32)
    _, o = lax.scan(step, state0, xs)
    return jnp.moveaxis(o, 0, 2).reshape(B, H, S, Dv)


def _gated_deltanet(x, w_in, conv_w, a_log, dt_bias, norm_g, w_out):
    B, S, _ = x.shape
    H, Dh = GDN_HEADS, GDN_HEAD_DIM
    HD = H * Dh
    qkv, z, beta_logit, a_logit = jnp.split(x @ w_in, [3 * HD, 4 * HD, 4 * HD + H], axis=-1)
    qkv = jax.nn.silu(_causal_depthwise_conv(qkv, conv_w))
    to_heads = lambda t: t.reshape(B, S, H, Dh).transpose(0, 2, 1, 3).astype(jnp.float32)
    q, k, v = (to_heads(t) for t in jnp.split(qkv, 3, axis=-1))
    q = _l2_normalize(q) * (Dh ** -0.5)
    k = _l2_normalize(k)
    beta = jax.nn.sigmoid(beta_logit.astype(jnp.float32)).transpose(0, 2, 1)
    g = (-jnp.exp(a_log.astype(jnp.float32))
         * jax.nn.softplus(a_logit.astype(jnp.float32) + dt_bias.astype(jnp.float32))).transpose(0, 2, 1)
    o = _chunk_gated_delta_rule(q, k, v, g, beta).transpose(0, 2, 1, 3)
    o = _rms_norm(o, norm_g.astype(jnp.float32)) * jax.nn.silu(z.reshape(B, S, H, Dh).astype(jnp.float32))
    return o.reshape(B, S, HD).astype(x.dtype) @ w_out


def _moe_swiglu(x, w_router, b_router, w_gu, w_down):
    B, S, D = x.shape
    xt = x.reshape(B * S, D)
    T = B * S
    logits = (xt @ w_router).astype(jnp.float32) + b_router.astype(jnp.float32)
    top_logit, top_idx = lax.top_k(logits, TOP_K)
    top_w = jax.nn.softmax(top_logit, axis=-1)
    flat_e = top_idx.reshape(-1)
    flat_tok = jnp.repeat(jnp.arange(T, dtype=jnp.int32), TOP_K)
    flat_w = top_w.reshape(-1)
    order = jnp.argsort(flat_e)
    sorted_e = flat_e[order]
    counts = jnp.bincount(flat_e, length=N_EXPERTS)
    padded = ((counts + MOE_BLOCK - 1) // MOE_BLOCK) * MOE_BLOCK
    pad_end = jnp.cumsum(padded)
    pad_start = pad_end - padded
    start = jnp.cumsum(counts) - counts
    dest = pad_start[sorted_e] + (jnp.arange(T * TOP_K) - start[sorted_e])
    n_rows = (-(-(T * TOP_K) // MOE_BLOCK) + N_EXPERTS) * MOE_BLOCK
    n_blocks = n_rows // MOE_BLOCK
    row_tok = jnp.zeros((n_rows,), jnp.int32).at[dest].set(flat_tok[order])
    row_w = jnp.zeros((n_rows,), jnp.float32).at[dest].set(flat_w[order])
    block_e = jnp.minimum(jnp.searchsorted(pad_end, jnp.arange(n_blocks) * MOE_BLOCK, side="right"),
                          N_EXPERTS - 1)
    xs = xt[row_tok].reshape(n_blocks, MOE_BLOCK, D)

    def expert_block(args):
        xb, e = args
        return _swiglu(xb, w_gu[e], w_down[e])

    y = lax.map(expert_block, (xs, block_e)).reshape(n_rows, D)
    out = jnp.zeros_like(xt).at[row_tok].add(y * row_w[:, None].astype(y.dtype))
    return out.reshape(B, S, D)


def setup_inputs(seed: int = 0) -> dict:
    key = jax.random.key(seed)
    ks = iter(jax.random.split(key, 64))
    f32 = jnp.float32

    def nrm(shape, scale):
        return jax.random.normal(next(ks), shape, f32) * scale

    def unif(shape, lo, hi):
        return jax.random.uniform(next(ks), shape, f32, lo, hi)

    D = D_MODEL
    HD = GDN_HEADS * GDN_HEAD_DIM
    inp = {}
    inp["x"] = nrm((BATCH, SEQ, D), 1.0)
    inp["p"] = nrm((DEPTH, BATCH, SEQ, PLE_DIM), 1.0)
    inp["ln_mix_g"] = 1.0 + nrm((DEPTH, D), 0.05)
    inp["ln_mix_b"] = nrm((DEPTH, D), 0.02)
    inp["ln_ffn_g"] = 1.0 + nrm((DEPTH, D), 0.05)
    inp["ln_ffn_b"] = nrm((DEPTH, D), 0.02)
    inp["ple_w"] = nrm((DEPTH, PLE_DIM, D), PLE_DIM ** -0.5 * BETA)
    inp["ple_gate_w"] = nrm((DEPTH, D, D), D ** -0.5)
    inp["fox_w_in"] = nrm((N_FOX, D, 3 * D + FOX_HEADS), D ** -0.5)
    inp["fox_b_f"] = unif((N_FOX, FOX_HEADS), 2.0, 5.0)
    inp["fox_w_out"] = nrm((N_FOX, D, D), D ** -0.5 * BETA)
    inp["lru_w_in"] = nrm((N_LRU, D, 2 * LRU_WIDTH), D ** -0.5)
    inp["lru_conv_w"] = nrm((N_LRU, LRU_CONV, LRU_WIDTH), LRU_CONV ** -0.5)
    inp["lru_conv_b"] = nrm((N_LRU, LRU_WIDTH), 0.02)
    inp["lru_w_a"] = nrm((N_LRU, LRU_BLOCKS, LRU_BLOCK_DIM, LRU_BLOCK_DIM), LRU_BLOCK_DIM ** -0.5)
    inp["lru_b_a"] = nrm((N_LRU, LRU_WIDTH), 0.02)
    inp["lru_w_x"] = nrm((N_LRU, LRU_BLOCKS, LRU_BLOCK_DIM, LRU_BLOCK_DIM), LRU_BLOCK_DIM ** -0.5)
    inp["lru_b_x"] = nrm((N_LRU, LRU_WIDTH), 0.02)
    a0 = unif((N_LRU, LRU_WIDTH), 0.9, 0.999)
    s0 = a0 ** (1.0 / LRU_C)
    inp["lru_lambda"] = jnp.log(s0) - jnp.log1p(-s0)
    inp["lru_w_out"] = nrm((N_LRU, LRU_WIDTH, D), LRU_WIDTH ** -0.5 * BETA)
    inp["cv_w_in"] = nrm((N_CONV, D, 2 * D), D ** -0.5)
    inp["cv_b_in"] = nrm((N_CONV, 2 * D), 0.02)
    inp["cv_dw_w"] = nrm((N_CONV, CONF_KERNEL, D), CONF_KERNEL ** -0.5)
    inp["cv_dw_b"] = nrm((N_CONV, D), 0.02)
    inp["cv_ln_g"] = 1.0 + nrm((N_CONV, D), 0.05)
    inp["cv_ln_b"] = nrm((N_CONV, D), 0.02)
    inp["cv_w_out"] = nrm((N_CONV, D, D), D ** -0.5 * BETA)
    inp["cv_b_out"] = nrm((N_CONV, D), 0.02)
    inp["gdn_w_in"] = jnp.concatenate([nrm((N_GDN, D, 4 * HD + GDN_HEADS), D ** -0.5),
                                       nrm((N_GDN, D, GDN_HEADS), 0.1 * D ** -0.5)], axis=-1)
    inp["gdn_conv_w"] = nrm((N_GDN, GDN_CONV, 3 * HD), GDN_CONV ** -0.5)
    inp["gdn_a_log"] = jnp.log(unif((N_GDN, GDN_HEADS), 1.0, 16.0))
    dt = jnp.exp(unif((N_GDN, GDN_HEADS), math.log(1e-3), math.log(1e-1)))
    inp["gdn_dt_bias"] = dt + jnp.log(-jnp.expm1(-dt))
    inp["gdn_norm_g"] = 1.0 + nrm((N_GDN, GDN_HEAD_DIM), 0.05)
    inp["gdn_w_out"] = nrm((N_GDN, HD, D), HD ** -0.5 * BETA)
    inp["ffn_w_gu"] = nrm((N_DENSE, D, 2 * D_FF), D ** -0.5)
    inp["ffn_w_down"] = nrm((N_DENSE, D_FF, D), D_FF ** -0.5 * BETA)
    inp["moe_w_router"] = nrm((N_MOE, D, N_EXPERTS), D ** -0.5)
    inp["moe_b_router"] = nrm((N_MOE, N_EXPERTS), 0.01)
    inp["moe_w_gu"] = nrm((N_MOE, N_EXPERTS, D, 2 * D_EXPERT), D ** -0.5)
    inp["moe_w_down"] = nrm((N_MOE, N_EXPERTS, D_EXPERT, D), D_EXPERT ** -0.5 * BETA)
    return inp


def reference(x, p, ln_mix_g, ln_mix_b, ln_ffn_g, ln_ffn_b, ple_w, ple_gate_w,
              fox_w_in, fox_b_f, fox_w_out,
              lru_w_in, lru_conv_w, lru_conv_b, lru_w_a, lru_b_a, lru_w_x, lru_b_x, lru_lambda, lru_w_out,
              cv_w_in, cv_b_in, cv_dw_w, cv_dw_b, cv_ln_g, cv_ln_b, cv_w_out, cv_b_out,
              gdn_w_in, gdn_conv_w, gdn_a_log, gdn_dt_bias, gdn_norm_g, gdn_w_out,
              ffn_w_gu, ffn_w_down,
              moe_w_router, moe_b_router, moe_w_gu, moe_w_down):
    for i in range(DEPTH):
        m = i % N_MIXERS
        j = i // N_MIXERS
        if m == 0:
            mix = _fox_attention(x, fox_w_in[j], fox_b_f[j], fox_w_out[j])
        elif m == 1:
            mix = _rglru_block(x, lru_w_in[j], lru_conv_w[j], lru_conv_b[j], lru_w_a[j], lru_b_a[j],
                               lru_w_x[j], lru_b_x[j], lru_lambda[j], lru_w_out[j])
        elif m == 2:
            mix = _conformer_conv(x, cv_w_in[j], cv_b_in[j], cv_dw_w[j], cv_dw_b[j],
                                  cv_ln_g[j], cv_ln_b[j], cv_w_out[j], cv_b_out[j])
        else:
            mix = _gated_deltanet(x, gdn_w_in[j], gdn_conv_w[j], gdn_a_log[j], gdn_dt_bias[j],
                                  gdn_norm_g[j], gdn_w_out[j])
        x = _layer_norm(ALPHA * x + mix, ln_mix_g[i], ln_mix_b[i])
        if i % 2 == 0:
            ff = _swiglu(x, ffn_w_gu[i // 2], ffn_w_down[i // 2])
        else:
            ff = _moe_swiglu(x, moe_w_router[i // 2], moe_b_router[i // 2], moe_w_gu[i // 2], moe_w_down[i // 2])
        x = _layer_norm(ALPHA * x + ff, ln_ffn_g[i], ln_ffn_b[i])
        gate = jax.nn.sigmoid(x @ ple_gate_w[i])
        x = x + gate * (p[i] @ ple_w[i])
    return x
```

```python
import functools
import math

import jax
import jax.numpy as jnp
from jax import lax
from jax.experimental import pallas as pl
from jax.experimental.pallas import tpu as pltpu

F32 = jnp.float32
BF16 = jnp.bfloat16

D_MODEL = 1024
DEPTH = 4
PLE_DIM = 256
ALPHA = (2 * DEPTH) ** 0.25
LN_EPS = 1e-5
RMS_EPS = 1e-6
FOX_HEADS = 16
FOX_HEAD_DIM = 64
LRU_WIDTH = 1280
LRU_BLOCKS = 10
LRU_CONV = 4
LRU_C = 8.0
CONF_KERNEL = 31
GDN_HEADS = 8
GDN_HEAD_DIM = 128
GDN_CONV = 4
GDN_CHUNK = 64
D_FF = 2816
N_EXPERTS = 8
TOP_K = 2
D_EXPERT = 3584

LANES = 128
VMEM_LIMIT = 56 * 1024 * 1024
NEG_BIG = -1e30


def _params(*sem):
    return pltpu.CompilerParams(dimension_semantics=sem, vmem_limit_bytes=VMEM_LIMIT)


def _dot(a, b):
    return jnp.dot(a, b, preferred_element_type=F32)


def _dot_nt(a, b):
    return lax.dot_general(a, b, (((1,), (1,)), ((), ())), preferred_element_type=F32)


def _dot_tn(a, b):
    return lax.dot_general(a, b, (((0,), (0,)), ((), ())), preferred_element_type=F32)


def _layer_norm(y, g, b):
    mu = jnp.mean(y, axis=-1, keepdims=True)
    d = y - mu
    var = jnp.mean(d * d, axis=-1, keepdims=True)
    return d * lax.rsqrt(var + LN_EPS) * g + b


def _softplus(x):
    return jnp.maximum(x, 0.0) + jnp.log1p(jnp.exp(-jnp.abs(x)))


def _lane_pick(x, lane_idx, idx):
    return jnp.sum(jnp.where(lane_idx == idx, x, 0.0), axis=-1, keepdims=True)


def _scan_rows(a, b, period):
    rows = b.shape[0]
    t = lax.broadcasted_iota(jnp.int32, b.shape, 0)
    if period < rows:
        t = t & (period - 1)
    d = 1
    while d < period:
        ok = t >= d
        b_sh = jnp.where(ok, pltpu.roll(b, d, axis=0), 0.0)
        if a is None:
            b = b + b_sh
        else:
            b = b + a * b_sh
            a = a * jnp.where(ok, pltpu.roll(a, d, axis=0), 1.0)
        d *= 2
    return b


def _proj_kernel(x_ref, w_ref, o_ref):
    o_ref[...] = _dot(x_ref[...].astype(BF16), w_ref[...]).astype(o_ref.dtype)


def _proj(x, w, out_dtype, tm=1024, tn=512):
    T, K = x.shape
    N = w.shape[1]
    tm = min(tm, T)
    tn = min(tn, N)
    return pl.pallas_call(
        _proj_kernel,
        grid=(T // tm, N // tn),
        in_specs=[pl.BlockSpec((tm, K), lambda i, j: (i, 0)),
                  pl.BlockSpec((K, tn), lambda i, j: (0, j))],
        out_specs=pl.BlockSpec((tm, tn), lambda i, j: (i, j)),
        out_shape=jax.ShapeDtypeStruct((T, N), out_dtype),
        compiler_params=_params("parallel", "arbitrary"),
        name="proj",
    )(x, w)


def _glu_proj_kernel(x_ref, wv_ref, wg_ref, bv_ref, bg_ref, o_ref):
    x = x_ref[...].astype(BF16)
    val = _dot(x, wv_ref[...]) + bv_ref[...]
    gate = _dot(x, wg_ref[...]) + bg_ref[...]
    o_ref[...] = (val * jax.nn.sigmoid(gate)).astype(o_ref.dtype)


def _glu_proj(x, w, b, out_dtype, tm=1024, tn=512):
    T, K = x.shape
    N = w.shape[1] // 2
    tm = min(tm, T)
    nj = N // tn
    b2 = b.reshape(1, 2 * N)
    return pl.pallas_call(
        _glu_proj_kernel,
        grid=(T // tm, nj),
        in_specs=[pl.BlockSpec((tm, K), lambda i, j: (i, 0)),
                  pl.BlockSpec((K, tn), lambda i, j: (0, j)),
                  pl.BlockSpec((K, tn), lambda i, j: (0, nj + j)),
                  pl.BlockSpec((1, tn), lambda i, j: (0, j)),
                  pl.BlockSpec((1, tn), lambda i, j: (0, nj + j))],
        out_specs=pl.BlockSpec((tm, tn), lambda i, j: (i, j)),
        out_shape=jax.ShapeDtypeStruct((T, N), out_dtype),
        compiler_params=_params("parallel", "arbitrary"),
        name="glu_proj",
    )(x, w, w, b2, b2)


def _out_ln_kernel(a_ref, w_ref, b_ref, res_ref, g_ref, beta_ref, o_ref, obf_ref):
    mix = _dot(a_ref[...], w_ref[...]) + b_ref[...]
    y = _layer_norm(ALPHA * res_ref[...] + mix, g_ref[...], beta_ref[...])
    o_ref[...] = y
    obf_ref[...] = y.astype(BF16)


def _out_ln_route_kernel(a_ref, w_ref, b_ref, res_ref, g_ref, beta_ref, wr_ref, br_ref,
                         o_ref, obf_ref, route_ref):
    mix = _dot(a_ref[...], w_ref[...]) + b_ref[...]
    y = _layer_norm(ALPHA * res_ref[...] + mix, g_ref[...], beta_ref[...])
    o_ref[...] = y
    obf_ref[...] = y.astype(BF16)
    logits = jnp.dot(y, wr_ref[...], preferred_element_type=F32,
                     precision=lax.Precision.HIGHEST) + br_ref[...]
    lane = lax.broadcasted_iota(jnp.int32, logits.shape, 1)
    logits = jnp.where(lane < N_EXPERTS, logits, -jnp.inf)
    l1 = jnp.max(logits, axis=-1, keepdims=True)
    i1 = jnp.min(jnp.where(logits == l1, lane, LANES), axis=-1, keepdims=True)
    rest = jnp.where(lane == i1, -jnp.inf, logits)
    l2 = jnp.max(rest, axis=-1, keepdims=True)
    i2 = jnp.min(jnp.where(rest == l2, lane, LANES), axis=-1, keepdims=True)
    e2 = jnp.exp(l2 - l1)
    w1 = 1.0 / (1.0 + e2)
    w2 = e2 / (1.0 + e2)
    route = jnp.where(lane == 0, i1.astype(F32),
                      jnp.where(lane == 1, i2.astype(F32),
                                jnp.where(lane == 2, w1, jnp.where(lane == 3, w2, 0.0))))
    route_ref[...] = route


def _out_ln(a, w, b, res, g, beta, router=None, tm=512):
    T, K = a.shape
    D = w.shape[1]
    row = lambda i: (i, 0)
    fixed = lambda i: (0, 0)
    in_specs = [pl.BlockSpec((tm, K), row), pl.BlockSpec((K, D), fixed), pl.BlockSpec((1, D), fixed),
                pl.BlockSpec((tm, D), row), pl.BlockSpec((1, D), fixed), pl.BlockSpec((1, D), fixed)]
    out_specs = [pl.BlockSpec((tm, D), row), pl.BlockSpec((tm, D), row)]
    out_shape = [jax.ShapeDtypeStruct((T, D), F32), jax.ShapeDtypeStruct((T, D), BF16)]
    args = [a, w, b.reshape(1, D), res, g.reshape(1, D), beta.reshape(1, D)]
    if router is None:
        body = _out_ln_kernel
    else:
        body = _out_ln_route_kernel
        w_r, b_r = router
        wr = jnp.zeros((D, LANES), F32).at[:, :N_EXPERTS].set(w_r)
        br = jnp.zeros((1, LANES), F32).at[0, :N_EXPERTS].set(b_r)
        in_specs += [pl.BlockSpec((D, LANES), fixed), pl.BlockSpec((1, LANES), fixed)]
        out_specs.append(pl.BlockSpec((tm, LANES), row))
        out_shape.append(jax.ShapeDtypeStruct((T, LANES), F32))
        args += [wr, br]
    return pl.pallas_call(
        body, grid=(T // tm,), in_specs=in_specs, out_specs=out_specs, out_shape=out_shape,
        compiler_params=_params("parallel"), name="out_ln",
    )(*args)


def _ln_ple(x1, ff, g, beta, wpg, p, wp):
    y = _layer_norm(ALPHA * x1 + ff, g, beta)
    gate = jax.nn.sigmoid(_dot(y.astype(BF16), wpg))
    return y + gate * _dot(p.astype(BF16), wp)


def _ffn_tail_kernel(x_ref, xbf_ref, wgu_ref, wd_ref, g_ref, beta_ref, wpg_ref, p_ref, wp_ref,
                     o_ref, obf_ref, *, tf):
    xb = xbf_ref[...]
    ff = None
    for c in range(D_FF // tf):
        gate = _dot(xb, wgu_ref[:, c * tf:(c + 1) * tf])
        up = _dot(xb, wgu_ref[:, D_FF + c * tf:D_FF + (c + 1) * tf])
        h = (jax.nn.silu(gate) * up).astype(BF16)
        part = _dot(h, wd_ref[c * tf:(c + 1) * tf, :])
        ff = part if ff is None else ff + part
    out = _ln_ple(x_ref[...], ff, g_ref[...], beta_ref[...], wpg_ref[...], p_ref[...], wp_ref[...])
    o_ref[...] = out
    obf_ref[...] = out.astype(BF16)


def _resident(shape):
    return pl.BlockSpec(shape, lambda i: (0,) * len(shape), pipeline_mode=pl.Buffered(1))


def _ffn_tail(x1, x1_bf, w_gu, w_down, g, beta, wpg, p, wp, tm=512, tf=256):
    T, D = x1.shape
    row = lambda i: (i, 0)
    return pl.pallas_call(
        functools.partial(_ffn_tail_kernel, tf=tf),
        grid=(T // tm,),
        in_specs=[pl.BlockSpec((tm, D), row), pl.BlockSpec((tm, D), row),
                  _resident((D, 2 * D_FF)), _resident((D_FF, D)),
                  _resident((1, D)), _resident((1, D)), _resident((D, D)),
                  pl.BlockSpec((tm, PLE_DIM), row), _resident((PLE_DIM, D))],
        out_specs=[pl.BlockSpec((tm, D), row), pl.BlockSpec((tm, D), row)],
        out_shape=[jax.ShapeDtypeStruct((T, D), F32), jax.ShapeDtypeStruct((T, D), BF16)],
        compiler_params=_params("parallel"),
        name="ffn_tail",
    )(x1, x1_bf, w_gu, w_down, g.reshape(1, D), beta.reshape(1, D), wpg, p, wp)


def _moe_tail_kernel(x_ref, y0_ref, y1_ref, g_ref, beta_ref, wpg_ref, p_ref, wp_ref, o_ref, obf_ref):
    ff = y0_ref[...] + y1_ref[...]
    out = _ln_ple(x_ref[...], ff, g_ref[...], beta_ref[...], wpg_ref[...], p_ref[...], wp_ref[...])
    o_ref[...] = out
    obf_ref[...] = out.astype(BF16)


def _moe_tail(x1, y0, y1, g, beta, wpg, p, wp, tm=512):
    T, D = x1.shape
    row = lambda i: (i, 0)
    return pl.pallas_call(
        _moe_tail_kernel,
        grid=(T // tm,),
        in_specs=[pl.BlockSpec((tm, D), row), pl.BlockSpec((tm, D), row), pl.BlockSpec((tm, D), row),
                  _resident((1, D)), _resident((1, D)), _resident((D, D)),
                  pl.BlockSpec((tm, PLE_DIM), row), _resident((PLE_DIM, D))],
        out_specs=[pl.BlockSpec((tm, D), row), pl.BlockSpec((tm, D), row)],
        out_shape=[jax.ShapeDtypeStruct((T, D), F32), jax.ShapeDtypeStruct((T, D), BF16)],
        compiler_params=_params("parallel"),
        name="moe_tail",
    )(x1, y0, y1, g.reshape(1, D), beta.reshape(1, D), wpg, p, wp)


def _fox_decay_kernel(f_ref, bf_ref, c_ref, ct_ref):
    x = f_ref[...] + bf_ref[...]
    c = _scan_rows(None, -_softplus(-x), x.shape[0])
    c_ref[...] = c
    ct_ref[...] = c.T


def _fox_decay(f_logit, b_f, batch, seq):
    return pl.pallas_call(
        _fox_decay_kernel,
        grid=(batch,),
        in_specs=[pl.BlockSpec((seq, LANES), lambda b: (b, 0)), pl.BlockSpec((1, LANES), lambda b: (0, 0))],
        out_specs=[pl.BlockSpec((seq, LANES), lambda b: (b, 0)),
                   pl.BlockSpec((None, LANES, seq), lambda b: (b, 0, 0))],
        out_shape=[jax.ShapeDtypeStruct((batch * seq, LANES), F32),
                   jax.ShapeDtypeStruct((batch, LANES, seq), F32)],
        compiler_params=_params("parallel"),
        name="fox_decay",
    )(f_logit, b_f)


def _fox_attn_kernel(q_ref, k_ref, v_ref, cq_ref, ck_ref, o_ref, *, tq):
    g = pl.program_id(1)
    qi = pl.program_id(2)
    lane = lax.broadcasted_iota(jnp.int32, (tq, LANES), 1)
    q = q_ref[...] * jnp.asarray(FOX_HEAD_DIM ** -0.5, BF16)
    cq = cq_ref[...]
    zero = jnp.zeros_like(q)
    qs = (jnp.where(lane < FOX_HEAD_DIM, q, zero), jnp.where(lane >= FOX_HEAD_DIM, q, zero))
    cqs = tuple(_lane_pick(cq, lane, 2 * g + hh) for hh in range(2))
    row = lax.broadcasted_iota(jnp.int32, (tq, tq), 0)
    col = lax.broadcasted_iota(jnp.int32, (tq, tq), 1)

    def block(j, carry, diagonal):
        start = pl.multiple_of(j * tq, tq)
        ks = k_ref[pl.ds(start, tq), :]
        vs = v_ref[pl.ds(start, tq), :]
        new = []
        for hh in range(2):
            m, l, acc = carry[hh]
            s = _dot_nt(qs[hh], ks) + cqs[hh] - ck_ref[pl.ds(2 * g + hh, 1), pl.ds(start, tq)]
            if diagonal:
                s = jnp.where(row >= col, s, NEG_BIG)
            m_new = jnp.maximum(m, jnp.max(s, axis=-1, keepdims=True))
            scale = jnp.exp(m - m_new)
            pr = jnp.exp(s - m_new)
            l = scale * l + jnp.sum(pr, axis=-1, keepdims=True)
            acc = scale * acc + _dot(pr.astype(BF16), vs)
            new.append((m_new, l, acc))
        return tuple(new)

    init = tuple((jnp.full((tq, 1), NEG_BIG, F32), jnp.zeros((tq, 1), F32), jnp.zeros((tq, LANES), F32))
                 for _ in range(2))
    carry = lax.fori_loop(0, qi, lambda j, c: block(j, c, False), init)
    (_, l0, a0), (_, l1, a1) = block(qi, carry, True)
    o_ref[...] = jnp.where(lane < FOX_HEAD_DIM, a0 / l0, a1 / l1).astype(o_ref.dtype)


def _fox_attn(qkv, c_col, c_row, batch, seq, tq=256):
    nq = seq // tq
    pairs = FOX_HEADS // 2
    return pl.pallas_call(
        functools.partial(_fox_attn_kernel, tq=tq),
        grid=(batch, pairs, nq),
        in_specs=[pl.BlockSpec((tq, LANES), lambda b, g, i: (b * nq + i, g)),
                  pl.BlockSpec((seq, LANES), lambda b, g, i: (b, pairs + g)),
                  pl.BlockSpec((seq, LANES), lambda b, g, i: (b, 2 * pairs + g)),
                  pl.BlockSpec((tq, LANES), lambda b, g, i: (b * nq + i, 0)),
                  pl.BlockSpec((None, LANES, seq), lambda b, g, i: (b, 0, 0))],
        out_specs=pl.BlockSpec((tq, LANES), lambda b, g, i: (b * nq + i, g)),
        out_shape=jax.ShapeDtypeStruct((batch * seq, D_MODEL), BF16),
        compiler_params=_params("parallel", "parallel", "arbitrary"),
        name="fox_attn",
    )(qkv, qkv, qkv, c_col, c_row)


def _pad_lanes(w):
    return jnp.zeros((w.shape[0], LANES), w.dtype).at[:, :w.shape[1]].set(w)


def _fox_mixer(x, x_bf, batch, seq, w_in, b_f, w_out, ln_g, ln_b, router=None):
    D = D_MODEL
    qkv = _proj(x_bf, w_in[:, :3 * D].astype(BF16), BF16)
    f_logit = _proj(x_bf, _pad_lanes(w_in[:, 3 * D:]).astype(BF16), F32, tn=LANES)
    c_col, c_row = _fox_decay(f_logit, _pad_lanes(b_f.reshape(1, -1)), batch, seq)
    o = _fox_attn(qkv, c_col, c_row, batch, seq)
    return _out_ln(o, w_out.astype(BF16), jnp.zeros((D,), F32), x, ln_g, ln_b, router)


PAD_ROWS = 8


def _causal_conv4(src, cw_ref, pad_ref):
    S = src.shape[0]
    pad_ref[0:PAD_ROWS, :] = jnp.zeros((PAD_ROWS, src.shape[1]), F32)
    pad_ref[PAD_ROWS:PAD_ROWS + S, :] = src
    out = None
    for k in range(4):
        off = PAD_ROWS - 3 + k
        term = cw_ref[k:k + 1, :] * pad_ref[off:off + S, :]
        out = term if out is None else out + term
    return out


def _lru_core_kernel(gate_ref, rec_ref, cw_ref, cb_ref, wa_ref, ba_ref, wx_ref, bx_ref, lam_ref, y_ref, pad_ref):
    S = rec_ref.shape[0]
    u = _causal_conv4(rec_ref[...].astype(F32), cw_ref, pad_ref) + cb_ref[...]
    ub = u.astype(BF16)
    r = jax.nn.sigmoid(_dot(ub, wa_ref[...]) + ba_ref[...])
    i = jax.nn.sigmoid(_dot(ub, wx_ref[...]) + bx_ref[...])
    log_a = -LRU_C * r * _softplus(-lam_ref[...])
    a = jnp.exp(log_a)
    th = jnp.tanh(log_a)
    b = jnp.sqrt(-2.0 * th / (1.0 - th)) * (i * u)
    h = _scan_rows(a, b, S)
    y_ref[...] = (jax.nn.gelu(gate_ref[...].astype(F32)) * h).astype(y_ref.dtype)


def _lru_core(gr, conv_w, conv_b, w_a, b_a, w_x, b_x, lam, batch, seq):
    nb = LRU_BLOCKS
    vec = lambda v: v.reshape(1, LRU_WIDTH)
    lane_blk = pl.BlockSpec((1, LANES), lambda b, n: (0, n))
    mat_blk = pl.BlockSpec((None, LANES, LANES), lambda b, n: (n, 0, 0))
    return pl.pallas_call(
        _lru_core_kernel,
        grid=(batch, nb),
        in_specs=[pl.BlockSpec((seq, LANES), lambda b, n: (b, n)),
                  pl.BlockSpec((seq, LANES), lambda b, n: (b, nb + n)),
                  pl.BlockSpec((LRU_CONV, LANES), lambda b, n: (0, n)),
                  lane_blk, mat_blk, lane_blk, mat_blk, lane_blk, lane_blk],
        out_specs=pl.BlockSpec((seq, LANES), lambda b, n: (b, n)),
        out_shape=jax.ShapeDtypeStruct((batch * seq, LRU_WIDTH), BF16),
        scratch_shapes=[pltpu.VMEM((seq + PAD_ROWS, LANES), F32)],
        compiler_params=_params("parallel", "parallel"),
        name="lru_core",
    )(gr, gr, conv_w, vec(conv_b), w_a.astype(BF16), vec(b_a), w_x.astype(BF16), vec(b_x), vec(lam))


def _lru_mixer(x, x_bf, batch, seq, w_in, conv_w, conv_b, w_a, b_a, w_x, b_x, lam, w_out, ln_g, ln_b,
               router=None):
    gr = _proj(x_bf, w_in.astype(BF16), BF16)
    y = _lru_core(gr, conv_w, conv_b, w_a, b_a, w_x, b_x, lam, batch, seq)
    return _out_ln(y, w_out.astype(BF16), jnp.zeros((D_MODEL,), F32), x, ln_g, ln_b, router)


CONF_HALO = 32


def _conf_tail_kernel(h_ref, halo_ref, dw_ref, dwb_ref, cg_ref, cb_ref, w_ref, b_ref, res_ref, g_ref, beta_ref,
                      o_ref, obf_ref, buf_ref, *, tt):
    first = pl.program_id(1) == 0
    halo = halo_ref[...].astype(F32)
    buf_ref[0:CONF_HALO, :] = jnp.where(first, 0.0, halo)
    buf_ref[CONF_HALO:CONF_HALO + tt, :] = h_ref[...].astype(F32)
    conv = None
    for k in range(CONF_KERNEL):
        off = CONF_HALO - (CONF_KERNEL - 1) + k
        term = dw_ref[k:k + 1, :] * buf_ref[off:off + tt, :]
        conv = term if conv is None else conv + term
    hs = jax.nn.silu(_layer_norm(conv + dwb_ref[...], cg_ref[...], cb_ref[...]))
    mix = _dot(hs.astype(BF16), w_ref[...]) + b_ref[...]
    y = _layer_norm(ALPHA * res_ref[...] + mix, g_ref[...], beta_ref[...])
    o_ref[...] = y
    obf_ref[...] = y.astype(BF16)


def _conf_tail(h, dw_w, dw_b, cg, cb, w_out, b_out, res, ln_g, ln_b, batch, seq, tt=256):
    D = D_MODEL
    nt = seq // tt
    per = tt // CONF_HALO
    row = lambda b, i: (b * nt + i, 0)
    fixed = lambda b, i: (0, 0)
    vec = lambda v: v.reshape(1, D)
    return pl.pallas_call(
        functools.partial(_conf_tail_kernel, tt=tt),
        grid=(batch, nt),
        in_specs=[pl.BlockSpec((tt, D), row),
                  pl.BlockSpec((CONF_HALO, D), lambda b, i: (jnp.maximum((b * nt + i) * per - 1, 0), 0)),
                  pl.BlockSpec((CONF_KERNEL, D), fixed), pl.BlockSpec((1, D), fixed),
                  pl.BlockSpec((1, D), fixed), pl.BlockSpec((1, D), fixed),
                  pl.BlockSpec((D, D), fixed), pl.BlockSpec((1, D), fixed),
                  pl.BlockSpec((tt, D), row), pl.BlockSpec((1, D), fixed), pl.BlockSpec((1, D), fixed)],
        out_specs=[pl.BlockSpec((tt, D), row), pl.BlockSpec((tt, D), row)],
        out_shape=[jax.ShapeDtypeStruct((batch * seq, D), F32), jax.ShapeDtypeStruct((batch * seq, D), BF16)],
        scratch_shapes=[pltpu.VMEM((CONF_HALO + tt, D), F32)],
        compiler_params=_params("parallel", "arbitrary"),
        name="conf_tail",
    )(h, h, dw_w, vec(dw_b), vec(cg), vec(cb), w_out, vec(b_out), res, vec(ln_g), vec(ln_b))


def _conf_mixer(x, x_bf, batch, seq, w_in, b_in, dw_w, dw_b, cg, cb, w_out, b_out, ln_g, ln_b):
    h = _glu_proj(x_bf, w_in.astype(BF16), b_in, BF16)
    return _conf_tail(h, dw_w, dw_b, cg, cb, w_out.astype(BF16), b_out, x, ln_g, ln_b, batch, seq)


def _gdn_core_kernel(q_ref, k_ref, v_ref, z_ref, ba_ref, cwq_ref, cwk_ref, cwv_ref, alog_ref, dtb_ref, ng_ref,
                     o_ref,
                     pad_ref, qs_ref, ks_ref, kb_ref, vb_ref, kbe_ref, gcs_ref,
                     u_ref, w_ref, kend_ref, qg_ref, attn_ref, egl_ref, *, hb):
    S = q_ref.shape[0]
    C = GDN_CHUNK
    Dh = GDN_HEAD_DIM
    n_chunks = S // C
    grp = pl.program_id(1)
    lane = lax.broadcasted_iota(jnp.int32, (S, LANES), 1)

    ba = ba_ref[...]
    beta_all = jax.nn.sigmoid(ba)
    g_all = -jnp.exp(alog_ref[...]) * _softplus(ba + dtb_ref[...])
    gc_all = _scan_rows(None, g_all, C)

    row = lax.broadcasted_iota(jnp.int32, (C, C), 0)
    col = lax.broadcasted_iota(jnp.int32, (C, C), 1)
    tril = row >= col
    strict = row > col
    eye = (row == col).astype(F32)
    lane_c = lax.broadcasted_iota(jnp.int32, (C, LANES), 1)

    def l2n(t):
        return t * lax.rsqrt(jnp.sum(t * t, axis=-1, keepdims=True) + 1e-6)

    for hh in range(hb):
        head = grp * hb + hh
        hs = slice(hh * Dh, (hh + 1) * Dh)
        beta = _lane_pick(beta_all, lane, head)
        gc = _lane_pick(gc_all, lane, GDN_HEADS + head)
        q = jax.nn.silu(_causal_conv4(q_ref[:, hs].astype(F32), cwq_ref.at[:, hs], pad_ref))
        k = jax.nn.silu(_causal_conv4(k_ref[:, hs].astype(F32), cwk_ref.at[:, hs], pad_ref))
        v = jax.nn.silu(_causal_conv4(v_ref[:, hs].astype(F32), cwv_ref.at[:, hs], pad_ref))
        q = l2n(q) * (Dh ** -0.5)
        k = l2n(k)
        eg = jnp.exp(gc)
        kb = k * beta
        qs_ref[...] = q.astype(BF16)
        ks_ref[...] = k.astype(BF16)
        kb_ref[...] = kb.astype(BF16)
        vb_ref[...] = (v * beta).astype(BF16)
        kbe_ref[...] = (kb * eg).astype(BF16)
        qg_ref[hh] = (q * eg).astype(BF16)
        gcs_ref[...] = jnp.broadcast_to(gc, (S, LANES))

        def prep(n, _):
            sl = pl.ds(pl.multiple_of(n * C, C), C)
            kc = ks_ref[sl, :]
            gcl = gcs_ref[sl, :]
            hi = gcl.astype(BF16).astype(F32)
            r1 = gcl - hi
            mid = r1.astype(BF16).astype(F32)
            lo = r1 - mid
            fa = jnp.where(lane_c == 0, hi, jnp.where(lane_c == 1, mid, jnp.where(
                lane_c == 2, lo, jnp.where(lane_c < 6, 1.0, 0.0))))
            fb = jnp.where(lane_c < 3, 1.0, jnp.where(lane_c == 3, -hi, jnp.where(
                lane_c == 4, -mid, jnp.where(lane_c == 5, -lo, 0.0))))
            gdiff = _dot_nt(fa.astype(BF16), fb.astype(BF16))
            decay = jnp.where(tril, jnp.exp(jnp.where(tril, gdiff, 0.0)), 0.0)
            lower = jnp.where(strict, _dot_nt(kb_ref[sl, :], kc) * decay, 0.0)
            inv = eye
            m = 1
            while m < C:
                pair = ((row // m) == (col // m) + 1) & ((row // (2 * m)) == (col // (2 * m)))
                off = jnp.where(pair, lower, 0.0).astype(BF16)
                inv_b = inv.astype(BF16)
                inv = inv - _dot(inv_b, _dot(off, inv_b).astype(BF16))
                m *= 2
            inv_b = inv.astype(BF16)
            u_ref[hh, sl, :] = _dot(inv_b, vb_ref[sl, :])
            w_ref[hh, sl, :] = _dot(inv_b, kbe_ref[sl, :]).astype(BF16)
            attn = jnp.where(tril, _dot_nt(qs_ref[sl, :], kc) * decay, 0.0)
            attn_ref[hh, sl, :] = attn.astype(BF16)
            gl = gcl[C - 1:C, :]
            kend_ref[hh, sl, :] = (kc.astype(F32) * jnp.exp(gl - gcl)).astype(BF16)
            egl_ref[hh, pl.ds(pl.multiple_of(n * 8, 8), 8), :] = jnp.broadcast_to(jnp.exp(gl), (8, LANES))
            return 0

        lax.fori_loop(0, n_chunks, prep, 0)

    ng = ng_ref[...]

    def step(n, states):
        sl = pl.ds(pl.multiple_of(n * C, C), C)
        new = []
        for hh in range(hb):
            st = states[hh]
            st_b = st.astype(BF16)
            v_new = u_ref[hh, sl, :] - _dot(w_ref[hh, sl, :], st_b)
            v_new_b = v_new.astype(BF16)
            o = _dot(qg_ref[hh, sl, :], st_b) + _dot(attn_ref[hh, sl, :], v_new_b)
            egl = egl_ref[hh, pl.ds(pl.multiple_of(n * 8, 8), 8), :][0:1, 0:1]
            new.append(st * egl + _dot_tn(kend_ref[hh, sl, :], v_new_b))
            o = o * lax.rsqrt(jnp.mean(o * o, axis=-1, keepdims=True) + RMS_EPS) * ng
            zc = z_ref[sl, hh * Dh:(hh + 1) * Dh].astype(F32)
            o_ref[sl, hh * Dh:(hh + 1) * Dh] = (o * jax.nn.silu(zc)).astype(o_ref.dtype)
        return tuple(new)

    lax.fori_loop(0, n_chunks, step, tuple(jnp.zeros((Dh, Dh), F32) for _ in range(hb)))


def _gdn_core(qkvz, ba, conv_w, a_log, dt_bias, norm_g, batch, seq, hb=2):
    H, Dh = GDN_HEADS, GDN_HEAD_DIM
    W = hb * Dh
    ng_ = H // hb
    blk = lambda part: pl.BlockSpec((seq, W), lambda b, g: (b, part * ng_ + g))
    cw = lambda part: pl.BlockSpec((GDN_CONV, W), lambda b, g: (0, part * ng_ + g))
    fixed = pl.BlockSpec((1, LANES), lambda b, g: (0, 0))
    pad = lambda v: jnp.zeros((1, LANES), F32).at[0, H:2 * H].set(v)
    return pl.pallas_call(
        functools.partial(_gdn_core_kernel, hb=hb),
        grid=(batch, ng_),
        in_specs=[blk(0), blk(1), blk(2), blk(3), pl.BlockSpec((seq, LANES), lambda b, g: (b, 0)),
                  cw(0), cw(1), cw(2), fixed, fixed, fixed],
        out_specs=pl.BlockSpec((seq, W), lambda b, g: (b, g)),
        out_shape=jax.ShapeDtypeStruct((batch * seq, H * Dh), BF16),
        scratch_shapes=[pltpu.VMEM((seq + PAD_ROWS, Dh), F32)]
                       + [pltpu.VMEM((seq, Dh), BF16)] * 5
                       + [pltpu.VMEM((seq, LANES), F32),
                          pltpu.VMEM((hb, seq, Dh), F32),
                          pltpu.VMEM((hb, seq, Dh), BF16),
                          pltpu.VMEM((hb, seq, Dh), BF16),
                          pltpu.VMEM((hb, seq, Dh), BF16),
                          pltpu.VMEM((hb, seq, GDN_CHUNK), BF16),
                          pltpu.VMEM((hb, seq // GDN_CHUNK * 8, LANES), F32)],
        compiler_params=_params("parallel", "parallel"),
        name="gdn_core",
    )(qkvz, qkvz, qkvz, qkvz, ba, conv_w, conv_w, conv_w, pad(a_log), pad(dt_bias), norm_g.reshape(1, Dh))


def _gdn_mixer(x, x_bf, batch, seq, w_in, conv_w, a_log, dt_bias, norm_g, w_out, ln_g, ln_b, router=None):
    HD = GDN_HEADS * GDN_HEAD_DIM
    qkvz = _proj(x_bf, w_in[:, :4 * HD].astype(BF16), BF16)
    ba = _proj(x_bf, _pad_lanes(w_in[:, 4 * HD:]).astype(BF16), F32, tn=LANES)
    o = _gdn_core(qkvz, ba, conv_w, a_log, dt_bias, norm_g, batch, seq)
    return _out_ln(o, w_out.astype(BF16), jnp.zeros((D_MODEL,), F32), x, ln_g, ln_b, router)


MOE_TILE = 1024


def _moe_expert_kernel(te_ref, tv_ref, x_ref, wg_ref, wu_ref, wd_ref, rw_ref, o_ref, acc_ref):
    i = pl.program_id(0)
    j = pl.program_id(1)
    last = pl.num_programs(1) - 1
    valid = tv_ref[i] == 1

    @pl.when(valid)
    def _():
        xb = x_ref[...]
        h = (jax.nn.silu(_dot(xb, wg_ref[...])) * _dot(xb, wu_ref[...])).astype(BF16)
        part = _dot(h, wd_ref[...])

        @pl.when(j == 0)
        def _():
            acc_ref[...] = part

        @pl.when(j > 0)
        def _():
            acc_ref[...] += part

        @pl.when(j == last)
        def _():
            o_ref[...] = acc_ref[...] * rw_ref[...]

    @pl.when(jnp.logical_not(valid) & (j == last))
    def _():
        o_ref[...] = jnp.zeros_like(o_ref)


def _moe_experts(xs, row_w, tile_e, tile_valid, w_gu, w_down, tf=512):
    n_rows, D = xs.shape
    n_tiles = n_rows // MOE_TILE
    nc = D_EXPERT // tf

    def chunk(j, tv, i):
        return jnp.where(tv[i] == 1, j, nc - 1)

    grid_spec = pltpu.PrefetchScalarGridSpec(
        num_scalar_prefetch=2,
        grid=(n_tiles, nc),
        in_specs=[pl.BlockSpec((MOE_TILE, D), lambda i, j, te, tv: (i, 0)),
                  pl.BlockSpec((None, D, tf), lambda i, j, te, tv: (te[i], 0, chunk(j, tv, i))),
                  pl.BlockSpec((None, D, tf), lambda i, j, te, tv: (te[i], 0, nc + chunk(j, tv, i))),
                  pl.BlockSpec((None, tf, D), lambda i, j, te, tv: (te[i], chunk(j, tv, i), 0)),
                  pl.BlockSpec((MOE_TILE, 1), lambda i, j, te, tv: (i, 0))],
        out_specs=pl.BlockSpec((MOE_TILE, D), lambda i, j, te, tv: (i, 0)),
        scratch_shapes=[pltpu.VMEM((MOE_TILE, D), F32)],
    )
    return pl.pallas_call(
        _moe_expert_kernel,
        grid_spec=grid_spec,
        out_shape=jax.ShapeDtypeStruct((n_rows, D), F32),
        compiler_params=_params("arbitrary", "arbitrary"),
        name="moe_experts",
    )(tile_e, tile_valid, xs, w_gu, w_gu, w_down, row_w)


def _moe_ffn(x1, x1_bf, route, w_gu, w_down, g, beta, wpg, p, wp):
    T, D = x1.shape
    n_pairs = T * TOP_K
    n_tiles = n_pairs // MOE_TILE + N_EXPERTS
    n_rows = n_tiles * MOE_TILE
    flat_e = route[:, :TOP_K].astype(jnp.int32).reshape(-1)
    flat_w = route[:, TOP_K:2 * TOP_K].reshape(-1)
    onehot = (flat_e[:, None] == jnp.arange(N_EXPERTS, dtype=jnp.int32)[None, :]).astype(jnp.int32)
    csum = jnp.cumsum(onehot, axis=0)
    counts = csum[-1]
    rank = jnp.sum((csum - 1) * onehot, axis=1)
    padded = ((counts + MOE_TILE - 1) // MOE_TILE) * MOE_TILE
    pad_end = jnp.cumsum(padded)
    pad_start = pad_end - padded
    dest = pad_start[flat_e] + rank
    row_tok = jnp.zeros((n_rows,), jnp.int32).at[dest].set(jnp.arange(n_pairs, dtype=jnp.int32) // TOP_K)
    row_w = jnp.zeros((n_rows,), F32).at[dest].set(flat_w)
    tile_start = jnp.arange(n_tiles, dtype=jnp.int32) * MOE_TILE
    tile_valid = tile_start < pad_end[-1]
    tile_e = jnp.minimum(jnp.searchsorted(pad_end, tile_start, side="right"), N_EXPERTS - 1).astype(jnp.int32)
    tile_e = jnp.where(tile_valid, tile_e, jnp.max(jnp.where(tile_valid, tile_e, 0)))
    xs = x1_bf[row_tok]
    y = _moe_experts(xs, row_w.reshape(n_rows, 1), tile_e, tile_valid.astype(jnp.int32), w_gu, w_down)
    dest2 = dest.reshape(T, TOP_K)
    return _moe_tail(x1, y[dest2[:, 0]], y[dest2[:, 1]], g, beta, wpg, p, wp)


def kernel(x, p, ln_mix_g, ln_mix_b, ln_ffn_g, ln_ffn_b, ple_w, ple_gate_w, fox_w_in, fox_b_f, fox_w_out, lru_w_in, lru_conv_w, lru_conv_b, lru_w_a, lru_b_a, lru_w_x, lru_b_x, lru_lambda, lru_w_out, cv_w_in, cv_b_in, cv_dw_w, cv_dw_b, cv_ln_g, cv_ln_b, cv_w_out, cv_b_out, gdn_w_in, gdn_conv_w, gdn_a_log, gdn_dt_bias, gdn_norm_g, gdn_w_out, ffn_w_gu, ffn_w_down, moe_w_router, moe_b_router, moe_w_gu, moe_w_down):
    B, S, D = x.shape
    T = B * S
    xf = x.reshape(T, D)
    xb = xf
    for i in range(DEPTH):
        m, j = i % 4, i // 4
        router = (moe_w_router[i // 2], moe_b_router[i // 2]) if i % 2 == 1 else None
        ln = (ln_mix_g[i], ln_mix_b[i])
        if m == 0:
            mixed = _fox_mixer(xf, xb, B, S, fox_w_in[j], fox_b_f[j], fox_w_out[j], *ln, router)
        elif m == 1:
            mixed = _lru_mixer(xf, xb, B, S, lru_w_in[j], lru_conv_w[j], lru_conv_b[j], lru_w_a[j], lru_b_a[j],
                               lru_w_x[j], lru_b_x[j], lru_lambda[j], lru_w_out[j], *ln, router)
        elif m == 2:
            mixed = _conf_mixer(xf, xb, B, S, cv_w_in[j], cv_b_in[j], cv_dw_w[j], cv_dw_b[j], cv_ln_g[j],
                                cv_ln_b[j], cv_w_out[j], cv_b_out[j], *ln)
        else:
            mixed = _gdn_mixer(xf, xb, B, S, gdn_w_in[j], gdn_conv_w[j], gdn_a_log[j], gdn_dt_bias[j],
                               gdn_norm_g[j], gdn_w_out[j], *ln, router)
        tail = (ln_ffn_g[i], ln_ffn_b[i], ple_gate_w[i].astype(BF16), p[i].reshape(T, PLE_DIM),
                ple_w[i].astype(BF16))
        if i % 2 == 0:
            x1, x1b = mixed
            xf, xb = _ffn_tail(x1, x1b, ffn_w_gu[i // 2].astype(BF16), ffn_w_down[i // 2].astype(BF16), *tail)
        else:
            x1, x1b, route = mixed
            xf, xb = _moe_ffn(x1, x1b, route, moe_w_gu[i // 2].astype(BF16), moe_w_down[i // 2].astype(BF16), *tail)
    return xf.reshape(B, S, D)
```

```python
import functools
import math

import jax
import jax.numpy as jnp
from jax import lax
from jax.experimental import pallas as pl
from jax.experimental.pallas import tpu as pltpu

F32 = jnp.float32
BF16 = jnp.bfloat16

D_MODEL = 1024
DEPTH = 4
PLE_DIM = 256
ALPHA = (2 * DEPTH) ** 0.25
LN_EPS = 1e-5
RMS_EPS = 1e-6
FOX_HEADS = 16
FOX_HEAD_DIM = 64
LRU_WIDTH = 1280
LRU_BLOCKS = 10
LRU_CONV = 4
LRU_C = 8.0
CONF_KERNEL = 31
GDN_HEADS = 8
GDN_HEAD_DIM = 128
GDN_CONV = 4
GDN_TILE = 256
D_FF = 2816
N_EXPERTS = 8
TOP_K = 2
D_EXPERT = 3584

LANES = 128
VMEM_LIMIT = 56 * 1024 * 1024
NEG_BIG = -1e30


def _params(*sem):
    return pltpu.CompilerParams(dimension_semantics=sem, vmem_limit_bytes=VMEM_LIMIT)


def _dot(a, b):
    return jnp.dot(a, b, preferred_element_type=F32)


def _dot_nt(a, b):
    return lax.dot_general(a, b, (((1,), (1,)), ((), ())), preferred_element_type=F32)


def _dot_tn(a, b):
    return lax.dot_general(a, b, (((0,), (0,)), ((), ())), preferred_element_type=F32)


def _layer_norm(y, g, b):
    mu = jnp.mean(y, axis=-1, keepdims=True)
    d = y - mu
    var = jnp.mean(d * d, axis=-1, keepdims=True)
    return d * lax.rsqrt(var + LN_EPS) * g + b


def _softplus(x):
    return jnp.maximum(x, 0.0) + jnp.log1p(jnp.exp(-jnp.abs(x)))


def _lane_pick(x, lane_idx, idx):
    return jnp.sum(jnp.where(lane_idx == idx, x, 0.0), axis=-1, keepdims=True)


def _scan_rows(a, b, period):
    rows = b.shape[0]
    t = lax.broadcasted_iota(jnp.int32, b.shape, 0)
    if period < rows:
        t = t & (period - 1)
    d = 1
    while d < period:
        ok = t >= d
        b_sh = jnp.where(ok, pltpu.roll(b, d, axis=0), 0.0)
        if a is None:
            b = b + b_sh
        else:
            b = b + a * b_sh
            a = a * jnp.where(ok, pltpu.roll(a, d, axis=0), 1.0)
        d *= 2
    return b


def _proj_kernel(x_ref, w_ref, o_ref):
    o_ref[...] = _dot(x_ref[...].astype(BF16), w_ref[...]).astype(o_ref.dtype)


def _proj(x, w, out_dtype, tm=1024, tn=512):
    T, K = x.shape
    N = w.shape[1]
    tm = min(tm, T)
    tn = min(tn, N)
    return pl.pallas_call(
        _proj_kernel,
        grid=(T // tm, N // tn),
        in_specs=[pl.BlockSpec((tm, K), lambda i, j: (i, 0)),
                  pl.BlockSpec((K, tn), lambda i, j: (0, j))],
        out_specs=pl.BlockSpec((tm, tn), lambda i, j: (i, j)),
        out_shape=jax.ShapeDtypeStruct((T, N), out_dtype),
        compiler_params=_params("parallel", "arbitrary"),
        name="proj",
    )(x, w)


def _glu_proj_kernel(x_ref, wv_ref, wg_ref, bv_ref, bg_ref, o_ref):
    x = x_ref[...].astype(BF16)
    val = _dot(x, wv_ref[...]) + bv_ref[...]
    gate = _dot(x, wg_ref[...]) + bg_ref[...]
    o_ref[...] = (val * jax.nn.sigmoid(gate)).astype(o_ref.dtype)


def _glu_proj(x, w, b, out_dtype, tm=1024, tn=512):
    T, K = x.shape
    N = w.shape[1] // 2
    tm = min(tm, T)
    nj = N // tn
    b2 = b.reshape(1, 2 * N)
    return pl.pallas_call(
        _glu_proj_kernel,
        grid=(T // tm, nj),
        in_specs=[pl.BlockSpec((tm, K), lambda i, j: (i, 0)),
                  pl.BlockSpec((K, tn), lambda i, j: (0, j)),
                  pl.BlockSpec((K, tn), lambda i, j: (0, nj + j)),
                  pl.BlockSpec((1, tn), lambda i, j: (0, j)),
                  pl.BlockSpec((1, tn), lambda i, j: (0, nj + j))],
        out_specs=pl.BlockSpec((tm, tn), lambda i, j: (i, j)),
        out_shape=jax.ShapeDtypeStruct((T, N), out_dtype),
        compiler_params=_params("parallel", "arbitrary"),
        name="glu_proj",
    )(x, w, w, b2, b2)


def _out_ln_kernel(a_ref, w_ref, b_ref, res_ref, g_ref, beta_ref, o_ref, obf_ref):
    mix = _dot(a_ref[...], w_ref[...]) + b_ref[...]
    y = _layer_norm(ALPHA * res_ref[...] + mix, g_ref[...], beta_ref[...])
    o_ref[...] = y
    obf_ref[...] = y.astype(BF16)


def _out_ln_route_kernel(a_ref, w_ref, b_ref, res_ref, g_ref, beta_ref, wr_ref, br_ref,
                         o_ref, obf_ref, route_ref):
    mix = _dot(a_ref[...], w_ref[...]) + b_ref[...]
    y = _layer_norm(ALPHA * res_ref[...] + mix, g_ref[...], beta_ref[...])
    o_ref[...] = y
    obf_ref[...] = y.astype(BF16)
    logits = jnp.dot(y, wr_ref[...], preferred_element_type=F32,
                     precision=lax.Precision.HIGHEST) + br_ref[...]
    lane = lax.broadcasted_iota(jnp.int32, logits.shape, 1)
    logits = jnp.where(lane < N_EXPERTS, logits, -jnp.inf)
    l1 = jnp.max(logits, axis=-1, keepdims=True)
    i1 = jnp.min(jnp.where(logits == l1, lane, LANES), axis=-1, keepdims=True)
    rest = jnp.where(lane == i1, -jnp.inf, logits)
    l2 = jnp.max(rest, axis=-1, keepdims=True)
    i2 = jnp.min(jnp.where(rest == l2, lane, LANES), axis=-1, keepdims=True)
    e2 = jnp.exp(l2 - l1)
    w1 = 1.0 / (1.0 + e2)
    w2 = e2 / (1.0 + e2)
    route = jnp.where(lane == 0, i1.astype(F32),
                      jnp.where(lane == 1, i2.astype(F32),
                                jnp.where(lane == 2, w1, jnp.where(lane == 3, w2, 0.0))))
    route_ref[...] = route


def _out_ln(a, w, b, res, g, beta, router=None, tm=512):
    T, K = a.shape
    D = w.shape[1]
    row = lambda i: (i, 0)
    fixed = lambda i: (0, 0)
    in_specs = [pl.BlockSpec((tm, K), row), pl.BlockSpec((K, D), fixed), pl.BlockSpec((1, D), fixed),
                pl.BlockSpec((tm, D), row), pl.BlockSpec((1, D), fixed), pl.BlockSpec((1, D), fixed)]
    out_specs = [pl.BlockSpec((tm, D), row), pl.BlockSpec((tm, D), row)]
    out_shape = [jax.ShapeDtypeStruct((T, D), F32), jax.ShapeDtypeStruct((T, D), BF16)]
    args = [a, w, b.reshape(1, D), res, g.reshape(1, D), beta.reshape(1, D)]
    if router is None:
        body = _out_ln_kernel
    else:
        body = _out_ln_route_kernel
        w_r, b_r = router
        wr = jnp.zeros((D, LANES), F32).at[:, :N_EXPERTS].set(w_r)
        br = jnp.zeros((1, LANES), F32).at[0, :N_EXPERTS].set(b_r)
        in_specs += [pl.BlockSpec((D, LANES), fixed), pl.BlockSpec((1, LANES), fixed)]
        out_specs.append(pl.BlockSpec((tm, LANES), row))
        out_shape.append(jax.ShapeDtypeStruct((T, LANES), F32))
        args += [wr, br]
    return pl.pallas_call(
        body, grid=(T // tm,), in_specs=in_specs, out_specs=out_specs, out_shape=out_shape,
        compiler_params=_params("parallel"), name="out_ln",
    )(*args)


def _ln_ple(x1, ff, g, beta, wpg, p, wp):
    y = _layer_norm(ALPHA * x1 + ff, g, beta)
    gate = jax.nn.sigmoid(_dot(y.astype(BF16), wpg))
    return y + gate * _dot(p.astype(BF16), wp)


def _ffn_tail_kernel(x_ref, xbf_ref, wgu_ref, wd_ref, g_ref, beta_ref, wpg_ref, p_ref, wp_ref,
                     o_ref, obf_ref, *, tf):
    xb = xbf_ref[...]
    ff = None
    for c in range(D_FF // tf):
        gate = _dot(xb, wgu_ref[:, c * tf:(c + 1) * tf])
        up = _dot(xb, wgu_ref[:, D_FF + c * tf:D_FF + (c + 1) * tf])
        h = (jax.nn.silu(gate) * up).astype(BF16)
        part = _dot(h, wd_ref[c * tf:(c + 1) * tf, :])
        ff = part if ff is None else ff + part
    out = _ln_ple(x_ref[...], ff, g_ref[...], beta_ref[...], wpg_ref[...], p_ref[...], wp_ref[...])
    o_ref[...] = out
    obf_ref[...] = out.astype(BF16)


def _resident(shape):
    return pl.BlockSpec(shape, lambda i: (0,) * len(shape), pipeline_mode=pl.Buffered(1))


def _ffn_tail(x1, x1_bf, w_gu, w_down, g, beta, wpg, p, wp, tm=512, tf=256):
    T, D = x1.shape
    row = lambda i: (i, 0)
    return pl.pallas_call(
        functools.partial(_ffn_tail_kernel, tf=tf),
        grid=(T // tm,),
        in_specs=[pl.BlockSpec((tm, D), row), pl.BlockSpec((tm, D), row),
                  _resident((D, 2 * D_FF)), _resident((D_FF, D)),
                  _resident((1, D)), _resident((1, D)), _resident((D, D)),
                  pl.BlockSpec((tm, PLE_DIM), row), _resident((PLE_DIM, D))],
        out_specs=[pl.BlockSpec((tm, D), row), pl.BlockSpec((tm, D), row)],
        out_shape=[jax.ShapeDtypeStruct((T, D), F32), jax.ShapeDtypeStruct((T, D), BF16)],
        compiler_params=_params("parallel"),
        name="ffn_tail",
    )(x1, x1_bf, w_gu, w_down, g.reshape(1, D), beta.reshape(1, D), wpg, p, wp)


def _moe_tail_kernel(x_ref, y0_ref, y1_ref, g_ref, beta_ref, wpg_ref, p_ref, wp_ref, o_ref, obf_ref):
    ff = y0_ref[...] + y1_ref[...]
    out = _ln_ple(x_ref[...], ff, g_ref[...], beta_ref[...], wpg_ref[...], p_ref[...], wp_ref[...])
    o_ref[...] = out
    obf_ref[...] = out.astype(BF16)


def _moe_tail(x1, y0, y1, g, beta, wpg, p, wp, tm=512):
    T, D = x1.shape
    row = lambda i: (i, 0)
    return pl.pallas_call(
        _moe_tail_kernel,
        grid=(T // tm,),
        in_specs=[pl.BlockSpec((tm, D), row), pl.BlockSpec((tm, D), row), pl.BlockSpec((tm, D), row),
                  _resident((1, D)), _resident((1, D)), _resident((D, D)),
                  pl.BlockSpec((tm, PLE_DIM), row), _resident((PLE_DIM, D))],
        out_specs=[pl.BlockSpec((tm, D), row), pl.BlockSpec((tm, D), row)],
        out_shape=[jax.ShapeDtypeStruct((T, D), F32), jax.ShapeDtypeStruct((T, D), BF16)],
        compiler_params=_params("parallel"),
        name="moe_tail",
    )(x1, y0, y1, g.reshape(1, D), beta.reshape(1, D), wpg, p, wp)


def _fox_decay_kernel(f_ref, bf_ref, ct_ref):
    x = f_ref[...] + bf_ref[...]
    c = _scan_rows(None, -_softplus(-x), x.shape[0])
    ct_ref[...] = c.T


def _fox_decay(f_logit, b_f, batch, seq):
    return pl.pallas_call(
        _fox_decay_kernel,
        grid=(batch,),
        in_specs=[pl.BlockSpec((seq, LANES), lambda b: (b, 0)), pl.BlockSpec((1, LANES), lambda b: (0, 0))],
        out_specs=pl.BlockSpec((None, LANES, seq), lambda b: (b, 0, 0)),
        out_shape=jax.ShapeDtypeStruct((batch, LANES, seq), F32),
        compiler_params=_params("parallel"),
        name="fox_decay",
    )(f_logit, b_f)


FOX_SUB = 128


def _fox_attn_kernel(q_ref, k_ref, v_ref, ck_ref, o_ref, *, tq):
    g = pl.program_id(1)
    qi = pl.program_id(2)
    ts = min(FOX_SUB, tq)
    nsub = tq // ts
    lane = lax.broadcasted_iota(jnp.int32, (ts, LANES), 1)
    chains = [(i, hh) for i in range(nsub) for hh in range(2)]
    qs = []
    for i, hh in chains:
        q = q_ref[i * ts:(i + 1) * ts, :] * jnp.asarray(FOX_HEAD_DIM ** -0.5, BF16)
        qs.append(jnp.where((lane >= FOX_HEAD_DIM) == (hh == 1), q, jnp.zeros_like(q)))

    def block(j, carry, diagonal):
        start = pl.multiple_of(j * tq, tq)
        width = [(i + 1) * ts if diagonal else tq for i in range(nsub)]
        qk = [_dot_nt(qs[c], k_ref[pl.ds(start, width[i]), :]) for c, (i, hh) in enumerate(chains)]
        ck = [ck_ref[pl.ds(2 * g + hh, 1), pl.ds(start, tq)] for hh in range(2)]
        stats = []
        for c, (i, hh) in enumerate(chains):
            m, l, _ = carry[c]
            s = qk[c] - ck[hh][:, :width[i]]
            if diagonal:
                row = lax.broadcasted_iota(jnp.int32, s.shape, 0)
                col = lax.broadcasted_iota(jnp.int32, s.shape, 1)
                s = jnp.where(col <= row + i * ts, s, NEG_BIG)
            m_new = jnp.maximum(m, jnp.max(s, axis=-1, keepdims=True))
            scale = jnp.exp(m - m_new)
            pr = jnp.exp(s - m_new)
            stats.append((m_new, scale * l + jnp.sum(pr, axis=-1, keepdims=True), scale, pr.astype(BF16)))
        pv = [_dot(stats[c][3], v_ref[pl.ds(start, width[i]), :]) for c, (i, hh) in enumerate(chains)]
        return tuple((stats[c][0], stats[c][1], stats[c][2] * carry[c][2] + pv[c]) for c in range(len(chains)))

    init = tuple((jnp.full((ts, 1), NEG_BIG, F32), jnp.zeros((ts, 1), F32), jnp.zeros((ts, LANES), F32))
                 for _ in chains)
    carry = lax.fori_loop(0, qi, lambda j, c: block(j, c, False), init)
    final = block(qi, carry, True)
    for i in range(nsub):
        (_, l0, a0), (_, l1, a1) = final[2 * i], final[2 * i + 1]
        o_ref[i * ts:(i + 1) * ts, :] = jnp.where(lane < FOX_HEAD_DIM, a0 / l0, a1 / l1).astype(o_ref.dtype)


def _fox_attn(qkv, c_row, batch, seq, tq=512):
    tq = min(tq, seq)
    nq = seq // tq
    pairs = FOX_HEADS // 2
    return pl.pallas_call(
        functools.partial(_fox_attn_kernel, tq=tq),
        grid=(batch, pairs, nq),
        in_specs=[pl.BlockSpec((tq, LANES), lambda b, g, i: (b * nq + i, g)),
                  pl.BlockSpec((seq, LANES), lambda b, g, i: (b, pairs + g)),
                  pl.BlockSpec((seq, LANES), lambda b, g, i: (b, 2 * pairs + g)),
                  pl.BlockSpec((None, LANES, seq), lambda b, g, i: (b, 0, 0))],
        out_specs=pl.BlockSpec((tq, LANES), lambda b, g, i: (b * nq + i, g)),
        out_shape=jax.ShapeDtypeStruct((batch * seq, D_MODEL), BF16),
        compiler_params=_params("parallel", "parallel", "arbitrary"),
        name="fox_attn",
    )(qkv, qkv, qkv, c_row)


def _pad_lanes(w):
    return jnp.zeros((w.shape[0], LANES), w.dtype).at[:, :w.shape[1]].set(w)


def _fox_mixer(x, x_bf, batch, seq, w_in, b_f, w_out, ln_g, ln_b, router=None):
    D = D_MODEL
    qkv = _proj(x_bf, w_in[:, :3 * D].astype(BF16), BF16)
    f_logit = _proj(x_bf, _pad_lanes(w_in[:, 3 * D:]).astype(BF16), F32, tn=LANES)
    c_row = _fox_decay(f_logit, _pad_lanes(b_f.reshape(1, -1)), batch, seq)
    o = _fox_attn(qkv, c_row, batch, seq)
    return _out_ln(o, w_out.astype(BF16), jnp.zeros((D,), F32), x, ln_g, ln_b, router)


PAD_ROWS = 8


def _causal_conv4(src, cw_ref, pad_ref):
    S = src.shape[0]
    pad_ref[0:PAD_ROWS, :] = jnp.zeros((PAD_ROWS, src.shape[1]), F32)
    pad_ref[PAD_ROWS:PAD_ROWS + S, :] = src
    out = None
    for k in range(4):
        off = PAD_ROWS - 3 + k
        term = cw_ref[k:k + 1, :] * pad_ref[off:off + S, :]
        out = term if out is None else out + term
    return out


def _lru_core_kernel(gate_ref, rec_ref, cw_ref, cb_ref, wa_ref, ba_ref, wx_ref, bx_ref, lam_ref, y_ref, pad_ref):
    S = rec_ref.shape[0]
    u = _causal_conv4(rec_ref[...].astype(F32), cw_ref, pad_ref) + cb_ref[...]
    ub = u.astype(BF16)
    r = jax.nn.sigmoid(_dot(ub, wa_ref[...]) + ba_ref[...])
    i = jax.nn.sigmoid(_dot(ub, wx_ref[...]) + bx_ref[...])
    log_a = -LRU_C * r * _softplus(-lam_ref[...])
    a = jnp.exp(log_a)
    th = jnp.tanh(log_a)
    b = jnp.sqrt(-2.0 * th / (1.0 - th)) * (i * u)
    h = _scan_rows(a, b, S)
    y_ref[...] = (jax.nn.gelu(gate_ref[...].astype(F32)) * h).astype(y_ref.dtype)


def _lru_core(gr, conv_w, conv_b, w_a, b_a, w_x, b_x, lam, batch, seq):
    nb = LRU_BLOCKS
    vec = lambda v: v.reshape(1, LRU_WIDTH)
    lane_blk = pl.BlockSpec((1, LANES), lambda b, n: (0, n))
    mat_blk = pl.BlockSpec((None, LANES, LANES), lambda b, n: (n, 0, 0))
    return pl.pallas_call(
        _lru_core_kernel,
        grid=(batch, nb),
        in_specs=[pl.BlockSpec((seq, LANES), lambda b, n: (b, n)),
                  pl.BlockSpec((seq, LANES), lambda b, n: (b, nb + n)),
                  pl.BlockSpec((LRU_CONV, LANES), lambda b, n: (0, n)),
                  lane_blk, mat_blk, lane_blk, mat_blk, lane_blk, lane_blk],
        out_specs=pl.BlockSpec((seq, LANES), lambda b, n: (b, n)),
        out_shape=jax.ShapeDtypeStruct((batch * seq, LRU_WIDTH), BF16),
        scratch_shapes=[pltpu.VMEM((seq + PAD_ROWS, LANES), F32)],
        compiler_params=_params("parallel", "parallel"),
        name="lru_core",
    )(gr, gr, conv_w, vec(conv_b), w_a.astype(BF16), vec(b_a), w_x.astype(BF16), vec(b_x), vec(lam))


def _lru_mixer(x, x_bf, batch, seq, w_in, conv_w, conv_b, w_a, b_a, w_x, b_x, lam, w_out, ln_g, ln_b,
               router=None):
    gr = _proj(x_bf, w_in.astype(BF16), BF16)
    y = _lru_core(gr, conv_w, conv_b, w_a, b_a, w_x, b_x, lam, batch, seq)
    return _out_ln(y, w_out.astype(BF16), jnp.zeros((D_MODEL,), F32), x, ln_g, ln_b, router)


CONF_HALO = 32


def _conf_tail_kernel(h_ref, halo_ref, dw_ref, dwb_ref, cg_ref, cb_ref, w_ref, b_ref, res_ref, g_ref, beta_ref,
                      o_ref, obf_ref, buf_ref, *, tt):
    first = pl.program_id(1) == 0
    halo = halo_ref[...].astype(F32)
    buf_ref[0:CONF_HALO, :] = jnp.where(first, 0.0, halo)
    buf_ref[CONF_HALO:CONF_HALO + tt, :] = h_ref[...].astype(F32)
    conv = None
    for k in range(CONF_KERNEL):
        off = CONF_HALO - (CONF_KERNEL - 1) + k
        term = dw_ref[k:k + 1, :] * buf_ref[off:off + tt, :]
        conv = term if conv is None else conv + term
    hs = jax.nn.silu(_layer_norm(conv + dwb_ref[...], cg_ref[...], cb_ref[...]))
    mix = _dot(hs.astype(BF16), w_ref[...]) + b_ref[...]
    y = _layer_norm(ALPHA * res_ref[...] + mix, g_ref[...], beta_ref[...])
    o_ref[...] = y
    obf_ref[...] = y.astype(BF16)


def _conf_tail(h, dw_w, dw_b, cg, cb, w_out, b_out, res, ln_g, ln_b, batch, seq, tt=256):
    D = D_MODEL
    nt = seq // tt
    per = tt // CONF_HALO
    row = lambda b, i: (b * nt + i, 0)
    fixed = lambda b, i: (0, 0)
    vec = lambda v: v.reshape(1, D)
    return pl.pallas_call(
        functools.partial(_conf_tail_kernel, tt=tt),
        grid=(batch, nt),
        in_specs=[pl.BlockSpec((tt, D), row),
                  pl.BlockSpec((CONF_HALO, D), lambda b, i: (jnp.maximum((b * nt + i) * per - 1, 0), 0)),
                  pl.BlockSpec((CONF_KERNEL, D), fixed), pl.BlockSpec((1, D), fixed),
                  pl.BlockSpec((1, D), fixed), pl.BlockSpec((1, D), fixed),
                  pl.BlockSpec((D, D), fixed), pl.BlockSpec((1, D), fixed),
                  pl.BlockSpec((tt, D), row), pl.BlockSpec((1, D), fixed), pl.BlockSpec((1, D), fixed)],
        out_specs=[pl.BlockSpec((tt, D), row), pl.BlockSpec((tt, D), row)],
        out_shape=[jax.ShapeDtypeStruct((batch * seq, D), F32), jax.ShapeDtypeStruct((batch * seq, D), BF16)],
        scratch_shapes=[pltpu.VMEM((CONF_HALO + tt, D), F32)],
        compiler_params=_params("parallel", "arbitrary"),
        name="conf_tail",
    )(h, h, dw_w, vec(dw_b), vec(cg), vec(cb), w_out, vec(b_out), res, vec(ln_g), vec(ln_b))


def _conf_mixer(x, x_bf, batch, seq, w_in, b_in, dw_w, dw_b, cg, cb, w_out, b_out, ln_g, ln_b):
    h = _glu_proj(x_bf, w_in.astype(BF16), b_in, BF16)
    return _conf_tail(h, dw_w, dw_b, cg, cb, w_out.astype(BF16), b_out, x, ln_g, ln_b, batch, seq)


def _gdn_core_kernel(q_ref, k_ref, v_ref, z_ref, ba_ref, cwq_ref, cwk_ref, cwv_ref, alog_ref, dtb_ref, ng_ref,
                     pm_ref, o_ref,
                     pad_ref, qs_ref, ks_ref, kb_ref, vb_ref, kbe_ref, gcs_ref,
                     u_ref, w_ref, kend_ref, qg_ref, attn_ref, egl_ref, *, hb, tile):
    S = q_ref.shape[0]
    C = tile
    Dh = GDN_HEAD_DIM
    n_chunks = S // C
    grp = pl.program_id(1)
    lane = lax.broadcasted_iota(jnp.int32, (S, LANES), 1)

    ba = ba_ref[...]
    beta_all = jax.nn.sigmoid(ba)
    g_all = -jnp.exp(alog_ref[...]) * _softplus(ba + dtb_ref[...])
    gc_all = _scan_rows(None, g_all, C)

    def l2n(t):
        return t * lax.rsqrt(jnp.sum(t * t, axis=-1, keepdims=True) + 1e-6)

    for hh in range(hb):
        head = grp * hb + hh
        hs = slice(hh * Dh, (hh + 1) * Dh)
        beta = _lane_pick(beta_all, lane, head)
        gc = _lane_pick(gc_all, lane, GDN_HEADS + head)
        q = jax.nn.silu(_causal_conv4(q_ref[:, hs].astype(F32), cwq_ref.at[:, hs], pad_ref))
        k = jax.nn.silu(_causal_conv4(k_ref[:, hs].astype(F32), cwk_ref.at[:, hs], pad_ref))
        v = jax.nn.silu(_causal_conv4(v_ref[:, hs].astype(F32), cwv_ref.at[:, hs], pad_ref))
        q = l2n(q) * (Dh ** -0.5)
        k = l2n(k)
        eg = jnp.exp(gc)
        kb = k * beta
        qs_ref[hh] = q.astype(BF16)
        ks_ref[hh] = k.astype(BF16)
        kb_ref[hh] = kb.astype(BF16)
        vb_ref[hh] = (v * beta).astype(BF16)
        kbe_ref[hh] = (kb * eg).astype(BF16)
        qg_ref[hh] = (q * eg).astype(BF16)
        gcs_ref[hh] = jnp.broadcast_to(gc, (S, LANES))

    row = lax.broadcasted_iota(jnp.int32, (C, C), 0)
    col = lax.broadcasted_iota(jnp.int32, (C, C), 1)
    lane_c = lax.broadcasted_iota(jnp.int32, (C, LANES), 1)
    n_levels = pm_ref.shape[0]
    chunks_per_step = 2 if n_chunks % 2 == 0 else 1

    def split_features(gcl):
        hi = gcl.astype(BF16).astype(F32)
        r1 = gcl - hi
        mid = r1.astype(BF16).astype(F32)
        lo = r1 - mid
        fa = jnp.where(lane_c == 0, hi, jnp.where(lane_c == 1, mid, jnp.where(
            lane_c == 2, lo, jnp.where(lane_c < 6, 1.0, 0.0))))
        fb = jnp.where(lane_c < 3, 1.0, jnp.where(lane_c == 3, -hi, jnp.where(
            lane_c == 4, -mid, jnp.where(lane_c == 5, -lo, 0.0))))
        return fa.astype(BF16), fb.astype(BF16)

    def prep(n, _):
        chains = [(hh, n * chunks_per_step + cc) for cc in range(chunks_per_step) for hh in range(hb)]
        sls = [pl.ds(pl.multiple_of(cn * C, C), C) for _, cn in chains]
        heads = [hh for hh, _ in chains]
        ch = range(len(chains))
        kc = [ks_ref[heads[i], sls[i], :] for i in ch]
        gcl = [gcs_ref[heads[i], sls[i], :] for i in ch]
        feats = [split_features(gcl[i]) for i in ch]
        gdiff = [_dot_nt(*feats[i]) for i in ch]
        kk = [_dot_nt(kb_ref[heads[i], sls[i], :], kc[i]) for i in ch]
        qk = [_dot_nt(qs_ref[heads[i], sls[i], :], kc[i]) for i in ch]
        tril = row >= col
        decay = [jnp.where(tril, jnp.exp(jnp.where(tril, gdiff[i], 0.0)), 0.0) for i in ch]
        lower = [(kk[i] * decay[i]).astype(BF16) for i in ch]
        for i in ch:
            attn_ref[heads[i], sls[i], :] = (qk[i] * decay[i]).astype(BF16)
        eye = jnp.where(row == col, 1.0, 0.0)
        inv = [eye - (lower[i] * pm_ref[0]).astype(F32) for i in ch]
        for lvl in range(1, n_levels):
            inv_b = [inv[i].astype(BF16) for i in ch]
            part = [_dot(lower[i] * pm_ref[lvl], inv_b[i]).astype(BF16) for i in ch]
            inv = [inv[i] - _dot(inv_b[i], part[i]) for i in ch]
        inv_b = [inv[i].astype(BF16) for i in ch]
        u = [_dot(inv_b[i], vb_ref[heads[i], sls[i], :]) for i in ch]
        w = [_dot(inv_b[i], kbe_ref[heads[i], sls[i], :]) for i in ch]
        for i in ch:
            hh, cn = chains[i]
            u_ref[hh, sls[i], :] = u[i]
            w_ref[hh, sls[i], :] = w[i].astype(BF16)
            gl = gcl[i][C - 1:C, :]
            kend_ref[hh, sls[i], :] = (kc[i].astype(F32) * jnp.exp(gl - gcl[i])).astype(BF16)
            egl_ref[hh, pl.ds(pl.multiple_of(cn * 8, 8), 8), :] = jnp.broadcast_to(jnp.exp(gl), (8, LANES))
        return 0

    lax.fori_loop(0, n_chunks // chunks_per_step, prep, 0)

    ng = ng_ref[...]

    def step(n, states):
        sl = pl.ds(pl.multiple_of(n * C, C), C)
        hr = range(hb)
        st_b = [states[hh].astype(BF16) for hh in hr]
        w_st = [_dot(w_ref[hh, sl, :], st_b[hh]) for hh in hr]
        q_st = [_dot(qg_ref[hh, sl, :], st_b[hh]) for hh in hr]
        v_new_b = [(u_ref[hh, sl, :] - w_st[hh]).astype(BF16) for hh in hr]
        intra = [_dot(attn_ref[hh, sl, :], v_new_b[hh]) for hh in hr]
        grow = [_dot_tn(kend_ref[hh, sl, :], v_new_b[hh]) for hh in hr]
        new = []
        for hh in hr:
            egl = egl_ref[hh, pl.ds(pl.multiple_of(n * 8, 8), 8), :][0:1, 0:1]
            new.append(states[hh] * egl + grow[hh])
            o = q_st[hh] + intra[hh]
            o = o * lax.rsqrt(jnp.mean(o * o, axis=-1, keepdims=True) + RMS_EPS) * ng
            zc = z_ref[sl, hh * Dh:(hh + 1) * Dh].astype(F32)
            o_ref[sl, hh * Dh:(hh + 1) * Dh] = (o * jax.nn.silu(zc)).astype(o_ref.dtype)
        return tuple(new)

    lax.fori_loop(0, n_chunks, step, tuple(jnp.zeros((Dh, Dh), F32) for _ in range(hb)))


def _pair_masks(tile):
    r = jnp.arange(tile)[:, None]
    c = jnp.arange(tile)[None, :]
    levels = []
    m = 1
    while m < tile:
        levels.append(((r // m) == (c // m) + 1) & ((r // (2 * m)) == (c // (2 * m))))
        m *= 2
    return jnp.stack(levels).astype(BF16)


def _gdn_core(qkvz, ba, conv_w, a_log, dt_bias, norm_g, batch, seq, hb=2, tile=GDN_TILE):
    H, Dh = GDN_HEADS, GDN_HEAD_DIM
    W = hb * Dh
    ng_ = H // hb
    tile = min(tile, seq)
    masks = _pair_masks(tile)
    blk = lambda part: pl.BlockSpec((seq, W), lambda b, g: (b, part * ng_ + g))
    cw = lambda part: pl.BlockSpec((GDN_CONV, W), lambda b, g: (0, part * ng_ + g))
    fixed = pl.BlockSpec((1, LANES), lambda b, g: (0, 0))
    pad = lambda v: jnp.zeros((1, LANES), F32).at[0, H:2 * H].set(v)
    return pl.pallas_call(
        functools.partial(_gdn_core_kernel, hb=hb, tile=tile),
        grid=(batch, ng_),
        in_specs=[blk(0), blk(1), blk(2), blk(3), pl.BlockSpec((seq, LANES), lambda b, g: (b, 0)),
                  cw(0), cw(1), cw(2), fixed, fixed, fixed,
                  pl.BlockSpec(masks.shape, lambda b, g: (0, 0, 0))],
        out_specs=pl.BlockSpec((seq, W), lambda b, g: (b, g)),
        out_shape=jax.ShapeDtypeStruct((batch * seq, H * Dh), BF16),
        scratch_shapes=[pltpu.VMEM((seq + PAD_ROWS, Dh), F32)]
                       + [pltpu.VMEM((hb, seq, Dh), BF16)] * 5
                       + [pltpu.VMEM((hb, seq, LANES), F32),
                          pltpu.VMEM((hb, seq, Dh), F32),
                          pltpu.VMEM((hb, seq, Dh), BF16),
                          pltpu.VMEM((hb, seq, Dh), BF16),
                          pltpu.VMEM((hb, seq, Dh), BF16),
                          pltpu.VMEM((hb, seq, tile), BF16),
                          pltpu.VMEM((hb, seq // tile * 8, LANES), F32)],
        compiler_params=_params("parallel", "parallel"),
        name="gdn_core",
    )(qkvz, qkvz, qkvz, qkvz, ba, conv_w, conv_w, conv_w, pad(a_log), pad(dt_bias), norm_g.reshape(1, Dh), masks)


def _gdn_mixer(x, x_bf, batch, seq, w_in, conv_w, a_log, dt_bias, norm_g, w_out, ln_g, ln_b, router=None):
    HD = GDN_HEADS * GDN_HEAD_DIM
    qkvz = _proj(x_bf, w_in[:, :4 * HD].astype(BF16), BF16)
    ba = _proj(x_bf, _pad_lanes(w_in[:, 4 * HD:]).astype(BF16), F32, tn=LANES)
    o = _gdn_core(qkvz, ba, conv_w, a_log, dt_bias, norm_g, batch, seq)
    return _out_ln(o, w_out.astype(BF16), jnp.zeros((D_MODEL,), F32), x, ln_g, ln_b, router)


MOE_TILE = 1024


def _moe_expert_kernel(te_ref, tv_ref, x_ref, wg_ref, wu_ref, wd_ref, rw_ref, o_ref, acc_ref, xb_ref):
    i = pl.program_id(0)
    j = pl.program_id(1)
    last = pl.num_programs(1) - 1
    valid = tv_ref[i] == 1

    @pl.when(valid & (j == 0))
    def _():
        xb_ref[...] = x_ref[...].astype(BF16)

    @pl.when(valid)
    def _():
        xb = xb_ref[...]
        h = (jax.nn.silu(_dot(xb, wg_ref[...])) * _dot(xb, wu_ref[...])).astype(BF16)
        part = _dot(h, wd_ref[...])

        @pl.when(j == 0)
        def _():
            acc_ref[...] = part

        @pl.when(j > 0)
        def _():
            acc_ref[...] += part

        @pl.when(j == last)
        def _():
            o_ref[...] = acc_ref[...] * rw_ref[...]

    @pl.when(jnp.logical_not(valid) & (j == last))
    def _():
        o_ref[...] = jnp.zeros_like(o_ref)


def _moe_experts(xs, row_w, tile_e, tile_valid, w_gu, w_down, tf=512):
    n_rows, D = xs.shape
    n_tiles = n_rows // MOE_TILE
    nc = D_EXPERT // tf

    def chunk(j, tv, i):
        return jnp.where(tv[i] == 1, j, nc - 1)

    grid_spec = pltpu.PrefetchScalarGridSpec(
        num_scalar_prefetch=2,
        grid=(n_tiles, nc),
        in_specs=[pl.BlockSpec((MOE_TILE, D), lambda i, j, te, tv: (i, 0)),
                  pl.BlockSpec((None, D, tf), lambda i, j, te, tv: (te[i], 0, chunk(j, tv, i))),
                  pl.BlockSpec((None, D, tf), lambda i, j, te, tv: (te[i], 0, nc + chunk(j, tv, i))),
                  pl.BlockSpec((None, tf, D), lambda i, j, te, tv: (te[i], chunk(j, tv, i), 0)),
                  pl.BlockSpec((MOE_TILE, 1), lambda i, j, te, tv: (i, 0))],
        out_specs=pl.BlockSpec((MOE_TILE, D), lambda i, j, te, tv: (i, 0)),
        scratch_shapes=[pltpu.VMEM((MOE_TILE, D), F32), pltpu.VMEM((MOE_TILE, D), BF16)],
    )
    return pl.pallas_call(
        _moe_expert_kernel,
        grid_spec=grid_spec,
        out_shape=jax.ShapeDtypeStruct((n_rows, D), F32),
        compiler_params=_params("arbitrary", "arbitrary"),
        name="moe_experts",
    )(tile_e, tile_valid, xs, w_gu, w_gu, w_down, row_w)


def _moe_ffn(x1, x1_bf, route, w_gu, w_down, g, beta, wpg, p, wp):
    T, D = x1.shape
    n_pairs = T * TOP_K
    n_tiles = n_pairs // MOE_TILE + N_EXPERTS
    n_rows = n_tiles * MOE_TILE
    flat_e = route[:, :TOP_K].astype(jnp.int32).reshape(-1)
    flat_w = route[:, TOP_K:2 * TOP_K].reshape(-1)
    onehot = (flat_e[:, None] == jnp.arange(N_EXPERTS, dtype=jnp.int32)[None, :]).astype(jnp.int32)
    csum = jnp.cumsum(onehot, axis=0)
    counts = csum[-1]
    rank = jnp.sum((csum - 1) * onehot, axis=1)
    padded = ((counts + MOE_TILE - 1) // MOE_TILE) * MOE_TILE
    pad_end = jnp.cumsum(padded)
    pad_start = pad_end - padded
    dest = pad_start[flat_e] + rank
    row_tok = jnp.zeros((n_rows,), jnp.int32).at[dest].set(jnp.arange(n_pairs, dtype=jnp.int32) // TOP_K)
    row_w = jnp.zeros((n_rows,), F32).at[dest].set(flat_w)
    tile_start = jnp.arange(n_tiles, dtype=jnp.int32) * MOE_TILE
    tile_valid = tile_start < pad_end[-1]
    tile_e = jnp.minimum(jnp.searchsorted(pad_end, tile_start, side="right"), N_EXPERTS - 1).astype(jnp.int32)
    tile_e = jnp.where(tile_valid, tile_e, jnp.max(jnp.where(tile_valid, tile_e, 0)))
    xs = x1[row_tok]
    y = _moe_experts(xs, row_w.reshape(n_rows, 1), tile_e, tile_valid.astype(jnp.int32), w_gu, w_down)
    dest2 = dest.reshape(T, TOP_K)
    return _moe_tail(x1, y[dest2[:, 0]], y[dest2[:, 1]], g, beta, wpg, p, wp)


def kernel(x, p, ln_mix_g, ln_mix_b, ln_ffn_g, ln_ffn_b, ple_w, ple_gate_w, fox_w_in, fox_b_f, fox_w_out, lru_w_in, lru_conv_w, lru_conv_b, lru_w_a, lru_b_a, lru_w_x, lru_b_x, lru_lambda, lru_w_out, cv_w_in, cv_b_in, cv_dw_w, cv_dw_b, cv_ln_g, cv_ln_b, cv_w_out, cv_b_out, gdn_w_in, gdn_conv_w, gdn_a_log, gdn_dt_bias, gdn_norm_g, gdn_w_out, ffn_w_gu, ffn_w_down, moe_w_router, moe_b_router, moe_w_gu, moe_w_down):
    B, S, D = x.shape
    T = B * S
    xf = x.reshape(T, D)
    xb = xf
    for i in range(DEPTH):
        m, j = i % 4, i // 4
        router = (moe_w_router[i // 2], moe_b_router[i // 2]) if i % 2 == 1 else None
        ln = (ln_mix_g[i], ln_mix_b[i])
        if m == 0:
            mixed = _fox_mixer(xf, xb, B, S, fox_w_in[j], fox_b_f[j], fox_w_out[j], *ln, router)
        elif m == 1:
            mixed = _lru_mixer(xf, xb, B, S, lru_w_in[j], lru_conv_w[j], lru_conv_b[j], lru_w_a[j], lru_b_a[j],
                               lru_w_x[j], lru_b_x[j], lru_lambda[j], lru_w_out[j], *ln, router)
        elif m == 2:
            mixed = _conf_mixer(xf, xb, B, S, cv_w_in[j], cv_b_in[j], cv_dw_w[j], cv_dw_b[j], cv_ln_g[j],
                                cv_ln_b[j], cv_w_out[j], cv_b_out[j], *ln)
        else:
            mixed = _gdn_mixer(xf, xb, B, S, gdn_w_in[j], gdn_conv_w[j], gdn_a_log[j], gdn_dt_bias[j],
                               gdn_norm_g[j], gdn_w_out[j], *ln, router)
        tail = (ln_ffn_g[i], ln_ffn_b[i], ple_gate_w[i].astype(BF16), p[i].reshape(T, PLE_DIM),
                ple_w[i].astype(BF16))
        if i % 2 == 0:
            x1, x1b = mixed
            xf, xb = _ffn_tail(x1, x1b, ffn_w_gu[i // 2].astype(BF16), ffn_w_down[i // 2].astype(BF16), *tail)
        else:
            x1, x1b, route = mixed
            xf, xb = _moe_ffn(x1, x1b, route, moe_w_gu[i // 2].astype(BF16), moe_w_down[i // 2].astype(BF16), *tail)
    return xf.reshape(B, S, D)
```

```python
import functools
import math

import jax
import jax.numpy as jnp
from jax import lax
from jax.experimental import pallas as pl
from jax.experimental.pallas import tpu as pltpu

F32 = jnp.float32
BF16 = jnp.bfloat16

D_MODEL = 1024
DEPTH = 4
PLE_DIM = 256
ALPHA = (2 * DEPTH) ** 0.25
LN_EPS = 1e-5
RMS_EPS = 1e-6
FOX_HEADS = 16
FOX_HEAD_DIM = 64
LRU_WIDTH = 1280
LRU_BLOCKS = 10
LRU_CONV = 4
LRU_C = 8.0
CONF_KERNEL = 31
GDN_HEADS = 8
GDN_HEAD_DIM = 128
GDN_CONV = 4
GDN_TILE = 256
D_FF = 2816
N_EXPERTS = 8
TOP_K = 2
D_EXPERT = 3584

LANES = 128
SUBLANES = 8
VMEM_LIMIT = 56 * 1024 * 1024
NEG_BIG = -1e30


def _params(*sem):
    return pltpu.CompilerParams(dimension_semantics=sem, vmem_limit_bytes=VMEM_LIMIT)


def _dot(a, b):
    return jnp.dot(a, b, preferred_element_type=F32)


def _dot_nt(a, b):
    return lax.dot_general(a, b, (((1,), (1,)), ((), ())), preferred_element_type=F32)


def _dot_tn(a, b):
    return lax.dot_general(a, b, (((0,), (0,)), ((), ())), preferred_element_type=F32)


def _layer_norm(y, g, b):
    mu = jnp.mean(y, axis=-1, keepdims=True)
    d = y - mu
    var = jnp.mean(d * d, axis=-1, keepdims=True)
    return d * lax.rsqrt(var + LN_EPS) * g + b


def _softplus(x):
    return jnp.maximum(x, 0.0) + jnp.log1p(jnp.exp(-jnp.abs(x)))


def _lane_pick(x, lane_idx, idx):
    return jnp.sum(jnp.where(lane_idx == idx, x, 0.0), axis=-1, keepdims=True)


def _scan_rows(a, b, period):
    rows = b.shape[0]
    t = lax.broadcasted_iota(jnp.int32, b.shape, 0)
    if period < rows:
        t = t & (period - 1)
    d = 1
    while d < period:
        if period == rows and d % SUBLANES == 0:
            if a is None:
                b = jnp.concatenate([b[:d], b[d:] + b[:-d]], axis=0)
            else:
                b = jnp.concatenate([b[:d], b[d:] + a[d:] * b[:-d]], axis=0)
                a = jnp.concatenate([a[:d], a[d:] * a[:-d]], axis=0)
            d *= 2
            continue
        ok = t >= d
        b_sh = jnp.where(ok, pltpu.roll(b, d, axis=0), 0.0)
        if a is None:
            b = b + b_sh
        else:
            b = b + a * b_sh
            a = a * jnp.where(ok, pltpu.roll(a, d, axis=0), 1.0)
        d *= 2
    return b


def _proj_kernel(x_ref, w_ref, o_ref):
    o_ref[...] = _dot(x_ref[...].astype(BF16), w_ref[...]).astype(o_ref.dtype)


def _proj(x, w, out_dtype, tm=1024, tn=512):
    T, K = x.shape
    N = w.shape[1]
    tm = min(tm, T)
    tn = min(tn, N)
    return pl.pallas_call(
        _proj_kernel,
        grid=(T // tm, N // tn),
        in_specs=[pl.BlockSpec((tm, K), lambda i, j: (i, 0)),
                  pl.BlockSpec((K, tn), lambda i, j: (0, j))],
        out_specs=pl.BlockSpec((tm, tn), lambda i, j: (i, j)),
        out_shape=jax.ShapeDtypeStruct((T, N), out_dtype),
        compiler_params=_params("parallel", "arbitrary"),
        name="proj",
    )(x, w)


def _glu_proj_kernel(x_ref, wv_ref, wg_ref, bv_ref, bg_ref, o_ref):
    x = x_ref[...].astype(BF16)
    val = _dot(x, wv_ref[...]) + bv_ref[...]
    gate = _dot(x, wg_ref[...]) + bg_ref[...]
    o_ref[...] = (val * jax.nn.sigmoid(gate)).astype(o_ref.dtype)


def _glu_proj(x, w, b, out_dtype, tm=1024, tn=512):
    T, K = x.shape
    N = w.shape[1] // 2
    tm = min(tm, T)
    nj = N // tn
    b2 = b.reshape(1, 2 * N)
    return pl.pallas_call(
        _glu_proj_kernel,
        grid=(T // tm, nj),
        in_specs=[pl.BlockSpec((tm, K), lambda i, j: (i, 0)),
                  pl.BlockSpec((K, tn), lambda i, j: (0, j)),
                  pl.BlockSpec((K, tn), lambda i, j: (0, nj + j)),
                  pl.BlockSpec((1, tn), lambda i, j: (0, j)),
                  pl.BlockSpec((1, tn), lambda i, j: (0, nj + j))],
        out_specs=pl.BlockSpec((tm, tn), lambda i, j: (i, j)),
        out_shape=jax.ShapeDtypeStruct((T, N), out_dtype),
        compiler_params=_params("parallel", "arbitrary"),
        name="glu_proj",
    )(x, w, w, b2, b2)


def _out_ln_kernel(a_ref, w_ref, b_ref, res_ref, g_ref, beta_ref, o_ref, obf_ref):
    mix = _dot(a_ref[...], w_ref[...]) + b_ref[...]
    y = _layer_norm(ALPHA * res_ref[...] + mix, g_ref[...], beta_ref[...])
    o_ref[...] = y
    obf_ref[...] = y.astype(BF16)


def _out_ln_route_kernel(a_ref, w_ref, b_ref, res_ref, g_ref, beta_ref, wr_ref, br_ref,
                         o_ref, obf_ref, route_ref):
    mix = _dot(a_ref[...], w_ref[...]) + b_ref[...]
    y = _layer_norm(ALPHA * res_ref[...] + mix, g_ref[...], beta_ref[...])
    o_ref[...] = y
    obf_ref[...] = y.astype(BF16)
    l1 = jnp.sum(y * wr_ref[0:1, :], axis=-1, keepdims=True) + br_ref[0]
    i1 = jnp.zeros_like(l1)
    l2 = jnp.full_like(l1, -jnp.inf)
    i2 = jnp.zeros_like(l1)
    for e in range(1, N_EXPERTS):
        v = jnp.sum(y * wr_ref[e:e + 1, :], axis=-1, keepdims=True) + br_ref[e]
        gt1 = v > l1
        gt2 = v > l2
        l2 = jnp.where(gt1, l1, jnp.where(gt2, v, l2))
        i2 = jnp.where(gt1, i1, jnp.where(gt2, float(e), i2))
        l1 = jnp.where(gt1, v, l1)
        i1 = jnp.where(gt1, float(e), i1)
    e2 = jnp.exp(l2 - l1)
    w1 = 1.0 / (1.0 + e2)
    w2 = e2 / (1.0 + e2)
    lane = lax.broadcasted_iota(jnp.int32, route_ref.shape, 1)
    route_ref[...] = jnp.where(lane == 0, i1, jnp.where(lane == 1, i2, jnp.where(
        lane == 2, w1, jnp.where(lane == 3, w2, 0.0))))


def _out_ln(a, w, b, res, g, beta, router=None, tm=512):
    T, K = a.shape
    D = w.shape[1]
    row = lambda i: (i, 0)
    fixed = lambda i: (0, 0)
    in_specs = [pl.BlockSpec((tm, K), row), pl.BlockSpec((K, D), fixed), pl.BlockSpec((1, D), fixed),
                pl.BlockSpec((tm, D), row), pl.BlockSpec((1, D), fixed), pl.BlockSpec((1, D), fixed)]
    out_specs = [pl.BlockSpec((tm, D), row), pl.BlockSpec((tm, D), row)]
    out_shape = [jax.ShapeDtypeStruct((T, D), F32), jax.ShapeDtypeStruct((T, D), BF16)]
    args = [a, w, b.reshape(1, D), res, g.reshape(1, D), beta.reshape(1, D)]
    if router is None:
        body = _out_ln_kernel
    else:
        body = _out_ln_route_kernel
        w_r, b_r = router
        in_specs += [pl.BlockSpec((N_EXPERTS, D), fixed), pl.BlockSpec(memory_space=pltpu.SMEM)]
        out_specs.append(pl.BlockSpec((tm, LANES), row))
        out_shape.append(jax.ShapeDtypeStruct((T, LANES), F32))
        args += [w_r.T, b_r]
    return pl.pallas_call(
        body, grid=(T // tm,), in_specs=in_specs, out_specs=out_specs, out_shape=out_shape,
        compiler_params=_params("parallel"), name="out_ln",
    )(*args)


def _ln_ple(x1, ff, g, beta, wpg, p, wp):
    y = _layer_norm(ALPHA * x1 + ff, g, beta)
    gate = jax.nn.sigmoid(_dot(y.astype(BF16), wpg))
    return y + gate * _dot(p.astype(BF16), wp)


def _ffn_tail_kernel(x_ref, xbf_ref, wgu_ref, wd_ref, g_ref, beta_ref, wpg_ref, p_ref, wp_ref,
                     o_ref, obf_ref, *, tf):
    xb = xbf_ref[...]
    ff = None
    for c in range(D_FF // tf):
        gate = _dot(xb, wgu_ref[:, c * tf:(c + 1) * tf])
        up = _dot(xb, wgu_ref[:, D_FF + c * tf:D_FF + (c + 1) * tf])
        h = (jax.nn.silu(gate) * up).astype(BF16)
        part = _dot(h, wd_ref[c * tf:(c + 1) * tf, :])
        ff = part if ff is None else ff + part
    out = _ln_ple(x_ref[...], ff, g_ref[...], beta_ref[...], wpg_ref[...], p_ref[...], wp_ref[...])
    o_ref[...] = out
    obf_ref[...] = out.astype(BF16)


def _resident(shape):
    return pl.BlockSpec(shape, lambda i: (0,) * len(shape), pipeline_mode=pl.Buffered(1))


def _ffn_tail(x1, x1_bf, w_gu, w_down, g, beta, wpg, p, wp, tm=512, tf=256):
    T, D = x1.shape
    row = lambda i: (i, 0)
    return pl.pallas_call(
        functools.partial(_ffn_tail_kernel, tf=tf),
        grid=(T // tm,),
        in_specs=[pl.BlockSpec((tm, D), row), pl.BlockSpec((tm, D), row),
                  _resident((D, 2 * D_FF)), _resident((D_FF, D)),
                  _resident((1, D)), _resident((1, D)), _resident((D, D)),
                  pl.BlockSpec((tm, PLE_DIM), row), _resident((PLE_DIM, D))],
        out_specs=[pl.BlockSpec((tm, D), row), pl.BlockSpec((tm, D), row)],
        out_shape=[jax.ShapeDtypeStruct((T, D), F32), jax.ShapeDtypeStruct((T, D), BF16)],
        compiler_params=_params("parallel"),
        name="ffn_tail",
    )(x1, x1_bf, w_gu, w_down, g.reshape(1, D), beta.reshape(1, D), wpg, p, wp)


def _moe_tail_kernel(x_ref, y0_ref, y1_ref, route_ref, g_ref, beta_ref, wpg_ref, p_ref, wp_ref, o_ref, obf_ref):
    route = route_ref[...]
    lane = lax.broadcasted_iota(jnp.int32, route.shape, 1)
    ff = (y0_ref[...] * _lane_pick(route, lane, TOP_K) + y1_ref[...] * _lane_pick(route, lane, TOP_K + 1))
    out = _ln_ple(x_ref[...], ff, g_ref[...], beta_ref[...], wpg_ref[...], p_ref[...], wp_ref[...])
    o_ref[...] = out
    obf_ref[...] = out.astype(BF16)


def _moe_tail(x1, y0, y1, route, g, beta, wpg, p, wp, tm=512):
    T, D = x1.shape
    row = lambda i: (i, 0)
    return pl.pallas_call(
        _moe_tail_kernel,
        grid=(T // tm,),
        in_specs=[pl.BlockSpec((tm, D), row), pl.BlockSpec((tm, D), row), pl.BlockSpec((tm, D), row),
                  pl.BlockSpec((tm, LANES), row), _resident((1, D)), _resident((1, D)), _resident((D, D)),
                  pl.BlockSpec((tm, PLE_DIM), row), _resident((PLE_DIM, D))],
        out_specs=[pl.BlockSpec((tm, D), row), pl.BlockSpec((tm, D), row)],
        out_shape=[jax.ShapeDtypeStruct((T, D), F32), jax.ShapeDtypeStruct((T, D), BF16)],
        compiler_params=_params("parallel"),
        name="moe_tail",
    )(x1, y0, y1, route, g.reshape(1, D), beta.reshape(1, D), wpg, p, wp)


def _fox_decay_kernel(f_ref, bf_ref, c_ref):
    x = f_ref[...] + bf_ref[...]
    c_ref[...] = _scan_rows(None, -_softplus(-x), x.shape[0])


def _fox_decay(f_logit, b_f, batch, seq):
    return pl.pallas_call(
        _fox_decay_kernel,
        grid=(batch,),
        in_specs=[pl.BlockSpec((seq, LANES), lambda b: (b, 0)), pl.BlockSpec((1, LANES), lambda b: (0, 0))],
        out_specs=pl.BlockSpec((seq, LANES), lambda b: (b, 0)),
        out_shape=jax.ShapeDtypeStruct((batch * seq, LANES), F32),
        compiler_params=_params("parallel"),
        name="fox_decay",
    )(f_logit, b_f)


FOX_SUB = 128


def _fox_attn_kernel(q_ref, k_ref, v_ref, c_ref, o_ref, ckb_ref, vt_ref, *, tq):
    g = pl.program_id(1)
    qi = pl.program_id(2)
    S = k_ref.shape[0]
    ts = min(FOX_SUB, tq)
    nsub = tq // ts

    @pl.when(qi == 0)
    def _():
        c = c_ref[...]
        lane_s = lax.broadcasted_iota(jnp.int32, (S, LANES), 1)
        for hh in range(2):
            ckb_ref[hh] = jnp.broadcast_to(_lane_pick(c, lane_s, 2 * g + hh), (S, LANES))
        vt_ref[...] = v_ref[...].astype(F32).T.astype(BF16)

    lane = lax.broadcasted_iota(jnp.int32, (ts, LANES), 1)
    chains = [(i, hh) for i in range(nsub) for hh in range(2)]
    qs = []
    for i, hh in chains:
        q = q_ref[i * ts:(i + 1) * ts, :] * jnp.asarray(FOX_HEAD_DIM ** -0.5, BF16)
        qs.append(jnp.where((lane >= FOX_HEAD_DIM) == (hh == 1), q, jnp.zeros_like(q)))

    def block(j, carry, diagonal):
        start = pl.multiple_of(j * tq, tq)
        width = [(i + 1) * ts if diagonal else tq for i in range(nsub)]
        qk = [_dot_nt(k_ref[pl.ds(start, width[i]), :], qs[c]) for c, (i, hh) in enumerate(chains)]
        stats = []
        for c, (i, hh) in enumerate(chains):
            m, l, _ = carry[c]
            s = qk[c] - ckb_ref[hh, pl.ds(start, width[i]), :]
            if diagonal:
                key = lax.broadcasted_iota(jnp.int32, s.shape, 0)
                qry = lax.broadcasted_iota(jnp.int32, s.shape, 1)
                s = jnp.where(key <= qry + i * ts, s, NEG_BIG)
            m_new = jnp.maximum(m, jnp.max(s, axis=0, keepdims=True))
            scale = jnp.exp(m - m_new)
            pr = jnp.exp(s - m_new)
            stats.append((m_new, scale * l + jnp.sum(pr, axis=0, keepdims=True), scale, pr.astype(BF16)))
        pv = [_dot(vt_ref[:, pl.ds(start, width[i])], stats[c][3]) for c, (i, hh) in enumerate(chains)]
        return tuple((stats[c][0], stats[c][1], stats[c][2] * carry[c][2] + pv[c]) for c in range(len(chains)))

    init = tuple((jnp.full((1, ts), NEG_BIG, F32), jnp.zeros((1, ts), F32), jnp.zeros((LANES, ts), F32))
                 for _ in chains)
    carry = lax.fori_loop(0, qi, lambda j, c: block(j, c, False), init)
    final = block(qi, carry, True)
    dim = lax.broadcasted_iota(jnp.int32, (LANES, ts), 0)
    for i in range(nsub):
        (_, l0, a0), (_, l1, a1) = final[2 * i], final[2 * i + 1]
        out_t = jnp.where(dim < FOX_HEAD_DIM, a0 / l0, a1 / l1)
        o_ref[i * ts:(i + 1) * ts, :] = out_t.T.astype(o_ref.dtype)


def _fox_attn(qkv, c, batch, seq, tq=512):
    tq = min(tq, seq)
    nq = seq // tq
    pairs = FOX_HEADS // 2
    return pl.pallas_call(
        functools.partial(_fox_attn_kernel, tq=tq),
        grid=(batch, pairs, nq),
        in_specs=[pl.BlockSpec((tq, LANES), lambda b, g, i: (b * nq + i, g)),
                  pl.BlockSpec((seq, LANES), lambda b, g, i: (b, pairs + g)),
                  pl.BlockSpec((seq, LANES), lambda b, g, i: (b, 2 * pairs + g)),
                  pl.BlockSpec((seq, LANES), lambda b, g, i: (b, 0))],
        out_specs=pl.BlockSpec((tq, LANES), lambda b, g, i: (b * nq + i, g)),
        out_shape=jax.ShapeDtypeStruct((batch * seq, D_MODEL), BF16),
        scratch_shapes=[pltpu.VMEM((2, seq, LANES), F32), pltpu.VMEM((LANES, seq), BF16)],
        compiler_params=_params("parallel", "parallel", "arbitrary"),
        name="fox_attn",
    )(qkv, qkv, qkv, c)


def _pad_lanes(w):
    return jnp.zeros((w.shape[0], LANES), w.dtype).at[:, :w.shape[1]].set(w)


def _fox_mixer(x, x_bf, batch, seq, w_in, b_f, w_out, ln_g, ln_b, router=None):
    D = D_MODEL
    qkv = _proj(x_bf, w_in[:, :3 * D].astype(BF16), BF16)
    f_logit = _proj(x_bf, _pad_lanes(w_in[:, 3 * D:]).astype(BF16), F32, tn=LANES)
    c = _fox_decay(f_logit, _pad_lanes(b_f.reshape(1, -1)), batch, seq)
    o = _fox_attn(qkv, c, batch, seq)
    return _out_ln(o, w_out.astype(BF16), jnp.zeros((D,), F32), x, ln_g, ln_b, router)


PAD_ROWS = 8


def _causal_conv4(src, cw_ref, pad_ref):
    S = src.shape[0]
    pad_ref[0:PAD_ROWS, :] = jnp.zeros((PAD_ROWS, src.shape[1]), F32)
    pad_ref[PAD_ROWS:PAD_ROWS + S, :] = src
    out = None
    for k in range(4):
        off = PAD_ROWS - 3 + k
        term = cw_ref[k:k + 1, :] * pad_ref[off:off + S, :]
        out = term if out is None else out + term
    return out


def _lru_core_kernel(gate_ref, rec_ref, cw_ref, cb_ref, wa_ref, ba_ref, wx_ref, bx_ref, lam_ref, y_ref, pad_ref):
    S = rec_ref.shape[0]
    u = _causal_conv4(rec_ref[...].astype(F32), cw_ref, pad_ref) + cb_ref[...]
    ub = u.astype(BF16)
    r = jax.nn.sigmoid(_dot(ub, wa_ref[...]) + ba_ref[...])
    i = jax.nn.sigmoid(_dot(ub, wx_ref[...]) + bx_ref[...])
    log_a = -LRU_C * r * _softplus(-lam_ref[...])
    a = jnp.exp(log_a)
    th = jnp.tanh(log_a)
    b = jnp.sqrt(-2.0 * th / (1.0 - th)) * (i * u)
    h = _scan_rows(a, b, S)
    y_ref[...] = (jax.nn.gelu(gate_ref[...].astype(F32)) * h).astype(y_ref.dtype)


def _lru_core(gr, conv_w, conv_b, w_a, b_a, w_x, b_x, lam, batch, seq):
    nb = LRU_BLOCKS
    vec = lambda v: v.reshape(1, LRU_WIDTH)
    lane_blk = pl.BlockSpec((1, LANES), lambda b, n: (0, n))
    mat_blk = pl.BlockSpec((None, LANES, LANES), lambda b, n: (n, 0, 0))
    return pl.pallas_call(
        _lru_core_kernel,
        grid=(batch, nb),
        in_specs=[pl.BlockSpec((seq, LANES), lambda b, n: (b, n)),
                  pl.BlockSpec((seq, LANES), lambda b, n: (b, nb + n)),
                  pl.BlockSpec((LRU_CONV, LANES), lambda b, n: (0, n)),
                  lane_blk, mat_blk, lane_blk, mat_blk, lane_blk, lane_blk],
        out_specs=pl.BlockSpec((seq, LANES), lambda b, n: (b, n)),
        out_shape=jax.ShapeDtypeStruct((batch * seq, LRU_WIDTH), BF16),
        scratch_shapes=[pltpu.VMEM((seq + PAD_ROWS, LANES), F32)],
        compiler_params=_params("parallel", "parallel"),
        name="lru_core",
    )(gr, gr, conv_w, vec(conv_b), w_a.astype(BF16), vec(b_a), w_x.astype(BF16), vec(b_x), vec(lam))


def _lru_mixer(x, x_bf, batch, seq, w_in, conv_w, conv_b, w_a, b_a, w_x, b_x, lam, w_out, ln_g, ln_b,
               router=None):
    gr = _proj(x_bf, w_in.astype(BF16), BF16)
    y = _lru_core(gr, conv_w, conv_b, w_a, b_a, w_x, b_x, lam, batch, seq)
    return _out_ln(y, w_out.astype(BF16), jnp.zeros((D_MODEL,), F32), x, ln_g, ln_b, router)


CONF_HALO = 32


def _conf_tail_kernel(h_ref, halo_ref, dw_ref, dwb_ref, cg_ref, cb_ref, w_ref, b_ref, res_ref, g_ref, beta_ref,
                      o_ref, obf_ref, buf_ref, *, tt):
    first = pl.program_id(1) == 0
    halo = halo_ref[...].astype(F32)
    buf_ref[0:CONF_HALO, :] = jnp.where(first, 0.0, halo)
    buf_ref[CONF_HALO:CONF_HALO + tt, :] = h_ref[...].astype(F32)
    first = CONF_HALO - (CONF_KERNEL - 1)
    conv = None
    for s in range(SUBLANES):
        offs = [o for o in range(first, first + CONF_KERNEL) if o % SUBLANES == s]
        span = offs[-1] - s + tt
        slab = buf_ref[s:s + span, :]
        part = None
        for o in offs:
            term = dw_ref[o - first:o - first + 1, :] * slab[o - s:o - s + tt]
            part = term if part is None else part + term
        conv = part if conv is None else conv + part
    hs = jax.nn.silu(_layer_norm(conv + dwb_ref[...], cg_ref[...], cb_ref[...]))
    mix = _dot(hs.astype(BF16), w_ref[...]) + b_ref[...]
    y = _layer_norm(ALPHA * res_ref[...] + mix, g_ref[...], beta_ref[...])
    o_ref[...] = y
    obf_ref[...] = y.astype(BF16)


def _conf_tail(h, dw_w, dw_b, cg, cb, w_out, b_out, res, ln_g, ln_b, batch, seq, tt=256):
    D = D_MODEL
    nt = seq // tt
    per = tt // CONF_HALO
    row = lambda b, i: (b * nt + i, 0)
    fixed = lambda b, i: (0, 0)
    vec = lambda v: v.reshape(1, D)
    return pl.pallas_call(
        functools.partial(_conf_tail_kernel, tt=tt),
        grid=(batch, nt),
        in_specs=[pl.BlockSpec((tt, D), row),
                  pl.BlockSpec((CONF_HALO, D), lambda b, i: (jnp.maximum((b * nt + i) * per - 1, 0), 0)),
                  pl.BlockSpec((CONF_KERNEL, D), fixed), pl.BlockSpec((1, D), fixed),
                  pl.BlockSpec((1, D), fixed), pl.BlockSpec((1, D), fixed),
                  pl.BlockSpec((D, D), fixed), pl.BlockSpec((1, D), fixed),
                  pl.BlockSpec((tt, D), row), pl.BlockSpec((1, D), fixed), pl.BlockSpec((1, D), fixed)],
        out_specs=[pl.BlockSpec((tt, D), row), pl.BlockSpec((tt, D), row)],
        out_shape=[jax.ShapeDtypeStruct((batch * seq, D), F32), jax.ShapeDtypeStruct((batch * seq, D), BF16)],
        scratch_shapes=[pltpu.VMEM((CONF_HALO + tt, D), F32)],
        compiler_params=_params("parallel", "arbitrary"),
        name="conf_tail",
    )(h, h, dw_w, vec(dw_b), vec(cg), vec(cb), w_out, vec(b_out), res, vec(ln_g), vec(ln_b))


def _conf_mixer(x, x_bf, batch, seq, w_in, b_in, dw_w, dw_b, cg, cb, w_out, b_out, ln_g, ln_b):
    h = _glu_proj(x_bf, w_in.astype(BF16), b_in, BF16)
    return _conf_tail(h, dw_w, dw_b, cg, cb, w_out.astype(BF16), b_out, x, ln_g, ln_b, batch, seq)


def _gdn_core_kernel(q_ref, k_ref, v_ref, z_ref, ba_ref, cwq_ref, cwk_ref, cwv_ref, alog_ref, dtb_ref, ng_ref,
                     pm_ref, o_ref,
                     pad_ref, qs_ref, ks_ref, kb_ref, vb_ref, kbe_ref, gcs_ref,
                     u_ref, w_ref, kend_ref, qg_ref, attn_ref, egl_ref, *, hb, tile):
    S = q_ref.shape[0]
    C = tile
    Dh = GDN_HEAD_DIM
    n_chunks = S // C
    grp = pl.program_id(1)
    lane = lax.broadcasted_iota(jnp.int32, (S, LANES), 1)

    ba = ba_ref[...]
    beta_all = jax.nn.sigmoid(ba)
    g_all = -jnp.exp(alog_ref[...]) * _softplus(ba + dtb_ref[...])
    gc_all = _scan_rows(None, g_all, C)

    def l2n(t):
        return t * lax.rsqrt(jnp.sum(t * t, axis=-1, keepdims=True) + 1e-6)

    for hh in range(hb):
        head = grp * hb + hh
        hs = slice(hh * Dh, (hh + 1) * Dh)
        beta = _lane_pick(beta_all, lane, head)
        gc = _lane_pick(gc_all, lane, GDN_HEADS + head)
        q = jax.nn.silu(_causal_conv4(q_ref[:, hs].astype(F32), cwq_ref.at[:, hs], pad_ref))
        k = jax.nn.silu(_causal_conv4(k_ref[:, hs].astype(F32), cwk_ref.at[:, hs], pad_ref))
        v = jax.nn.silu(_causal_conv4(v_ref[:, hs].astype(F32), cwv_ref.at[:, hs], pad_ref))
        q = l2n(q) * (Dh ** -0.5)
        k = l2n(k)
        eg = jnp.exp(gc)
        kb = k * beta
        qs_ref[hh] = q.astype(BF16)
        ks_ref[hh] = k.astype(BF16)
        kb_ref[hh] = kb.astype(BF16)
        vb_ref[hh] = (v * beta).astype(BF16)
        kbe_ref[hh] = (kb * eg).astype(BF16)
        qg_ref[hh] = (q * eg).astype(BF16)
        gcs_ref[hh] = jnp.broadcast_to(gc, (S, LANES))

    row = lax.broadcasted_iota(jnp.int32, (C, C), 0)
    col = lax.broadcasted_iota(jnp.int32, (C, C), 1)
    lane_c = lax.broadcasted_iota(jnp.int32, (C, LANES), 1)
    n_levels = pm_ref.shape[0]
    chunks_per_step = 2 if n_chunks % 2 == 0 else 1

    def split_features(gcl):
        hi = gcl.astype(BF16).astype(F32)
        r1 = gcl - hi
        mid = r1.astype(BF16).astype(F32)
        lo = r1 - mid
        fa = jnp.where(lane_c == 0, hi, jnp.where(lane_c == 1, mid, jnp.where(
            lane_c == 2, lo, jnp.where(lane_c < 6, 1.0, 0.0))))
        fb = jnp.where(lane_c < 3, 1.0, jnp.where(lane_c == 3, -hi, jnp.where(
            lane_c == 4, -mid, jnp.where(lane_c == 5, -lo, 0.0))))
        return fa.astype(BF16), fb.astype(BF16)

    def prep(n, _):
        chains = [(hh, n * chunks_per_step + cc) for cc in range(chunks_per_step) for hh in range(hb)]
        sls = [pl.ds(pl.multiple_of(cn * C, C), C) for _, cn in chains]
        heads = [hh for hh, _ in chains]
        ch = range(len(chains))
        kc = [ks_ref[heads[i], sls[i], :] for i in ch]
        gcl = [gcs_ref[heads[i], sls[i], :] for i in ch]
        feats = [split_features(gcl[i]) for i in ch]
        gdiff = [_dot_nt(*feats[i]) for i in ch]
        kk = [_dot_nt(kb_ref[heads[i], sls[i], :], kc[i]) for i in ch]
        qk = [_dot_nt(qs_ref[heads[i], sls[i], :], kc[i]) for i in ch]
        tril = row >= col
        decay = [jnp.where(tril, jnp.exp(jnp.where(tril, gdiff[i], 0.0)), 0.0) for i in ch]
        lower = [(kk[i] * decay[i]).astype(BF16) for i in ch]
        for i in ch:
            attn_ref[heads[i], sls[i], :] = (qk[i] * decay[i]).astype(BF16)
        eye = jnp.where(row == col, 1.0, 0.0)
        inv = [eye - (lower[i] * pm_ref[0]).astype(F32) for i in ch]
        for lvl in range(1, n_levels):
            inv_b = [inv[i].astype(BF16) for i in ch]
            part = [_dot(lower[i] * pm_ref[lvl], inv_b[i]).astype(BF16) for i in ch]
            inv = [inv[i] - _dot(inv_b[i], part[i]) for i in ch]
        inv_b = [inv[i].astype(BF16) for i in ch]
        u = [_dot(inv_b[i], vb_ref[heads[i], sls[i], :]) for i in ch]
        w = [_dot(inv_b[i], kbe_ref[heads[i], sls[i], :]) for i in ch]
        for i in ch:
            hh, cn = chains[i]
            u_ref[hh, sls[i], :] = u[i]
            w_ref[hh, sls[i], :] = w[i].astype(BF16)
            gl = gcl[i][C - 1:C, :]
            kend_ref[hh, sls[i], :] = (kc[i].astype(F32) * jnp.exp(gl - gcl[i])).astype(BF16)
            egl_ref[hh, pl.ds(pl.multiple_of(cn * 8, 8), 8), :] = jnp.broadcast_to(jnp.exp(gl), (8, LANES))
        return 0

    lax.fori_loop(0, n_chunks // chunks_per_step, prep, 0)

    ng = ng_ref[...]

    def step(n, states):
        sl = pl.ds(pl.multiple_of(n * C, C), C)
        hr = range(hb)
        st_b = [states[hh].astype(BF16) for hh in hr]
        w_st = [_dot(w_ref[hh, sl, :], st_b[hh]) for hh in hr]
        q_st = [_dot(qg_ref[hh, sl, :], st_b[hh]) for hh in hr]
        v_new_b = [(u_ref[hh, sl, :] - w_st[hh]).astype(BF16) for hh in hr]
        intra = [_dot(attn_ref[hh, sl, :], v_new_b[hh]) for hh in hr]
        grow = [_dot_tn(kend_ref[hh, sl, :], v_new_b[hh]) for hh in hr]
        new = []
        for hh in hr:
            egl = egl_ref[hh, pl.ds(pl.multiple_of(n * 8, 8), 8), :][0:1, 0:1]
            new.append(states[hh] * egl + grow[hh])
            o = q_st[hh] + intra[hh]
            o = o * lax.rsqrt(jnp.mean(o * o, axis=-1, keepdims=True) + RMS_EPS) * ng
            zc = z_ref[sl, hh * Dh:(hh + 1) * Dh].astype(F32)
            o_ref[sl, hh * Dh:(hh + 1) * Dh] = (o * jax.nn.silu(zc)).astype(o_ref.dtype)
        return tuple(new)

    lax.fori_loop(0, n_chunks, step, tuple(jnp.zeros((Dh, Dh), F32) for _ in range(hb)))


def _pair_masks(tile):
    r = jnp.arange(tile)[:, None]
    c = jnp.arange(tile)[None, :]
    levels = []
    m = 1
    while m < tile:
        levels.append(((r // m) == (c // m) + 1) & ((r // (2 * m)) == (c // (2 * m))))
        m *= 2
    return jnp.stack(levels).astype(BF16)


def _gdn_core(qkvz, ba, conv_w, a_log, dt_bias, norm_g, batch, seq, hb=2, tile=GDN_TILE):
    H, Dh = GDN_HEADS, GDN_HEAD_DIM
    W = hb * Dh
    ng_ = H // hb
    tile = min(tile, seq)
    masks = _pair_masks(tile)
    blk = lambda part: pl.BlockSpec((seq, W), lambda b, g: (b, part * ng_ + g))
    cw = lambda part: pl.BlockSpec((GDN_CONV, W), lambda b, g: (0, part * ng_ + g))
    fixed = pl.BlockSpec((1, LANES), lambda b, g: (0, 0))
    pad = lambda v: jnp.zeros((1, LANES), F32).at[0, H:2 * H].set(v)
    return pl.pallas_call(
        functools.partial(_gdn_core_kernel, hb=hb, tile=tile),
        grid=(batch, ng_),
        in_specs=[blk(0), blk(1), blk(2), blk(3), pl.BlockSpec((seq, LANES), lambda b, g: (b, 0)),
                  cw(0), cw(1), cw(2), fixed, fixed, fixed,
                  pl.BlockSpec(masks.shape, lambda b, g: (0, 0, 0))],
        out_specs=pl.BlockSpec((seq, W), lambda b, g: (b, g)),
        out_shape=jax.ShapeDtypeStruct((batch * seq, H * Dh), BF16),
        scratch_shapes=[pltpu.VMEM((seq + PAD_ROWS, Dh), F32)]
                       + [pltpu.VMEM((hb, seq, Dh), BF16)] * 5
                       + [pltpu.VMEM((hb, seq, LANES), F32),
                          pltpu.VMEM((hb, seq, Dh), F32),
                          pltpu.VMEM((hb, seq, Dh), BF16),
                          pltpu.VMEM((hb, seq, Dh), BF16),
                          pltpu.VMEM((hb, seq, Dh), BF16),
                          pltpu.VMEM((hb, seq, tile), BF16),
                          pltpu.VMEM((hb, seq // tile * 8, LANES), F32)],
        compiler_params=_params("parallel", "parallel"),
        name="gdn_core",
    )(qkvz, qkvz, qkvz, qkvz, ba, conv_w, conv_w, conv_w, pad(a_log), pad(dt_bias), norm_g.reshape(1, Dh), masks)


def _gdn_mixer(x, x_bf, batch, seq, w_in, conv_w, a_log, dt_bias, norm_g, w_out, ln_g, ln_b, router=None):
    HD = GDN_HEADS * GDN_HEAD_DIM
    qkvz = _proj(x_bf, w_in[:, :4 * HD].astype(BF16), BF16)
    ba = _proj(x_bf, _pad_lanes(w_in[:, 4 * HD:]).astype(BF16), F32, tn=LANES)
    o = _gdn_core(qkvz, ba, conv_w, a_log, dt_bias, norm_g, batch, seq)
    return _out_ln(o, w_out.astype(BF16), jnp.zeros((D_MODEL,), F32), x, ln_g, ln_b, router)


MOE_TILE = 1024


def _moe_expert_kernel(te_ref, tv_ref, x_ref, wg_ref, wu_ref, wd_ref, o_ref, xb_ref):
    i = pl.program_id(0)
    j = pl.program_id(1)
    valid = tv_ref[i] == 1
    half = MOE_TILE // 2
    halves = [pl.ds(0, half), pl.ds(half, half)]

    @pl.when(j == 0)
    def _():
        xb_ref[...] = x_ref[...].astype(BF16)
        o_ref[...] = jnp.zeros_like(o_ref)

    @pl.when(valid)
    def _():
        wg = wg_ref[...].astype(BF16)
        wu = wu_ref[...].astype(BF16)
        wd = wd_ref[...].astype(BF16)
        gate = [_dot(xb_ref[h, :], wg) for h in halves]
        up = [_dot(xb_ref[h, :], wu) for h in halves]
        hid = [(jax.nn.silu(gate[k]) * up[k]).astype(BF16) for k in range(2)]
        for k, h in enumerate(halves):
            o_ref[h, :] += _dot(hid[k], wd)


def _moe_experts(xs, tile_e, tile_valid, w_gu, w_down, tf=512):
    n_rows, D = xs.shape
    n_tiles = n_rows // MOE_TILE
    nc = D_EXPERT // tf

    def chunk(j, tv, i):
        return jnp.where(tv[i] == 1, j, nc - 1)

    grid_spec = pltpu.PrefetchScalarGridSpec(
        num_scalar_prefetch=2,
        grid=(n_tiles, nc),
        in_specs=[pl.BlockSpec((MOE_TILE, D), lambda i, j, te, tv: (i, 0)),
                  pl.BlockSpec((None, D, tf), lambda i, j, te, tv: (te[i], 0, chunk(j, tv, i))),
                  pl.BlockSpec((None, D, tf), lambda i, j, te, tv: (te[i], 0, nc + chunk(j, tv, i))),
                  pl.BlockSpec((None, tf, D), lambda i, j, te, tv: (te[i], chunk(j, tv, i), 0))],
        out_specs=pl.BlockSpec((MOE_TILE, D), lambda i, j, te, tv: (i, 0)),
        scratch_shapes=[pltpu.VMEM((MOE_TILE, D), BF16)],
    )
    return pl.pallas_call(
        _moe_expert_kernel,
        grid_spec=grid_spec,
        out_shape=jax.ShapeDtypeStruct((n_rows, D), F32),
        compiler_params=_params("arbitrary", "arbitrary"),
        name="moe_experts",
    )(tile_e, tile_valid, xs, w_gu, w_gu, w_down)


def _moe_ffn(x1, route, w_gu, w_down, g, beta, wpg, p, wp):
    T, D = x1.shape
    n_pairs = T * TOP_K
    n_tiles = n_pairs // MOE_TILE + N_EXPERTS
    n_rows = n_tiles * MOE_TILE
    flat_e = route[:, :TOP_K].astype(jnp.int32).reshape(-1)
    onehot = (flat_e[:, None] == jnp.arange(N_EXPERTS, dtype=jnp.int32)[None, :]).astype(jnp.int32)
    csum = jnp.cumsum(onehot, axis=0)
    counts = csum[-1]
    rank = jnp.sum((csum - 1) * onehot, axis=1)
    padded = ((counts + MOE_TILE - 1) // MOE_TILE) * MOE_TILE
    pad_end = jnp.cumsum(padded)
    pad_start = pad_end - padded
    dest = pad_start[flat_e] + rank
    row_tok = jnp.zeros((n_rows,), jnp.int32).at[dest].set(jnp.arange(n_pairs, dtype=jnp.int32) // TOP_K)
    tile_start = jnp.arange(n_tiles, dtype=jnp.int32) * MOE_TILE
    tile_valid = tile_start < pad_end[-1]
    tile_e = jnp.minimum(jnp.searchsorted(pad_end, tile_start, side="right"), N_EXPERTS - 1).astype(jnp.int32)
    tile_e = jnp.where(tile_valid, tile_e, jnp.max(jnp.where(tile_valid, tile_e, 0)))
    xs = x1[row_tok]
    y = _moe_experts(xs, tile_e, tile_valid.astype(jnp.int32), w_gu, w_down)
    dest2 = dest.reshape(T, TOP_K)
    return _moe_tail(x1, y[dest2[:, 0]], y[dest2[:, 1]], route, g, beta, wpg, p, wp)


def kernel(x, p, ln_mix_g, ln_mix_b, ln_ffn_g, ln_ffn_b, ple_w, ple_gate_w, fox_w_in, fox_b_f, fox_w_out, lru_w_in, lru_conv_w, lru_conv_b, lru_w_a, lru_b_a, lru_w_x, lru_b_x, lru_lambda, lru_w_out, cv_w_in, cv_b_in, cv_dw_w, cv_dw_b, cv_ln_g, cv_ln_b, cv_w_out, cv_b_out, gdn_w_in, gdn_conv_w, gdn_a_log, gdn_dt_bias, gdn_norm_g, gdn_w_out, ffn_w_gu, ffn_w_down, moe_w_router, moe_b_router, moe_w_gu, moe_w_down):
    B, S, D = x.shape
    T = B * S
    xf = x.reshape(T, D)
    xb = xf
    for i in range(DEPTH):
        m, j = i % 4, i // 4
        router = (moe_w_router[i // 2], moe_b_router[i // 2]) if i % 2 == 1 else None
        ln = (ln_mix_g[i], ln_mix_b[i])
        if m == 0:
            mixed = _fox_mixer(xf, xb, B, S, fox_w_in[j], fox_b_f[j], fox_w_out[j], *ln, router)
        elif m == 1:
            mixed = _lru_mixer(xf, xb, B, S, lru_w_in[j], lru_conv_w[j], lru_conv_b[j], lru_w_a[j], lru_b_a[j],
                               lru_w_x[j], lru_b_x[j], lru_lambda[j], lru_w_out[j], *ln, router)
        elif m == 2:
            mixed = _conf_mixer(xf, xb, B, S, cv_w_in[j], cv_b_in[j], cv_dw_w[j], cv_dw_b[j], cv_ln_g[j],
                                cv_ln_b[j], cv_w_out[j], cv_b_out[j], *ln)
        else:
            mixed = _gdn_mixer(xf, xb, B, S, gdn_w_in[j], gdn_conv_w[j], gdn_a_log[j], gdn_dt_bias[j],
                               gdn_norm_g[j], gdn_w_out[j], *ln, router)
        tail = (ln_ffn_g[i], ln_ffn_b[i], ple_gate_w[i].astype(BF16), p[i].reshape(T, PLE_DIM),
                ple_w[i].astype(BF16))
        if i % 2 == 0:
            x1, x1b = mixed
            xf, xb = _ffn_tail(x1, x1b, ffn_w_gu[i // 2].astype(BF16), ffn_w_down[i // 2].astype(BF16), *tail)
        else:
            x1, x1b, route = mixed
            xf, xb = _moe_ffn(x1, route, moe_w_gu[i // 2], moe_w_down[i // 2], *tail)
    return xf.reshape(B, S, D)
```

```python
import functools
import math

import jax
import jax.numpy as jnp
from jax import lax
from jax.experimental import pallas as pl
from jax.experimental.pallas import tpu as pltpu

F32 = jnp.float32
BF16 = jnp.bfloat16

D_MODEL = 1024
DEPTH = 4
PLE_DIM = 256
ALPHA = (2 * DEPTH) ** 0.25
LN_EPS = 1e-5
RMS_EPS = 1e-6
FOX_HEADS = 16
FOX_HEAD_DIM = 64
LRU_WIDTH = 1280
LRU_BLOCKS = 10
LRU_CONV = 4
LRU_C = 8.0
CONF_KERNEL = 31
GDN_HEADS = 8
GDN_HEAD_DIM = 128
GDN_CONV = 4
GDN_TILE = 256
D_FF = 2816
N_EXPERTS = 8
TOP_K = 2
D_EXPERT = 3584

LANES = 128
SUBLANES = 8
VMEM_LIMIT = 56 * 1024 * 1024
NEG_BIG = -1e30


def _params(*sem):
    return pltpu.CompilerParams(dimension_semantics=sem, vmem_limit_bytes=VMEM_LIMIT)


def _dot(a, b):
    return jnp.dot(a, b, preferred_element_type=F32)


def _dot_nt(a, b):
    return lax.dot_general(a, b, (((1,), (1,)), ((), ())), preferred_element_type=F32)


def _dot_tn(a, b):
    return lax.dot_general(a, b, (((0,), (0,)), ((), ())), preferred_element_type=F32)


def _layer_norm(y, g, b):
    mu = jnp.mean(y, axis=-1, keepdims=True)
    d = y - mu
    var = jnp.mean(d * d, axis=-1, keepdims=True)
    return d * lax.rsqrt(var + LN_EPS) * g + b


def _softplus(x):
    return jnp.maximum(x, 0.0) + jnp.log1p(jnp.exp(-jnp.abs(x)))


def _lane_pick(x, lane_idx, idx):
    return jnp.sum(jnp.where(lane_idx == idx, x, 0.0), axis=-1, keepdims=True)


def _scan_rows(a, b, period):
    rows = b.shape[0]
    t = lax.broadcasted_iota(jnp.int32, b.shape, 0)
    if period < rows:
        t = t & (period - 1)
    d = 1
    while d < period:
        if period == rows and d % SUBLANES == 0:
            if a is None:
                b = jnp.concatenate([b[:d], b[d:] + b[:-d]], axis=0)
            else:
                b = jnp.concatenate([b[:d], b[d:] + a[d:] * b[:-d]], axis=0)
                a = jnp.concatenate([a[:d], a[d:] * a[:-d]], axis=0)
            d *= 2
            continue
        ok = t >= d
        b_sh = jnp.where(ok, pltpu.roll(b, d, axis=0), 0.0)
        if a is None:
            b = b + b_sh
        else:
            b = b + a * b_sh
            a = a * jnp.where(ok, pltpu.roll(a, d, axis=0), 1.0)
        d *= 2
    return b


def _proj_kernel(x_ref, w_ref, o_ref):
    o_ref[...] = _dot(x_ref[...].astype(BF16), w_ref[...]).astype(o_ref.dtype)


def _proj(x, w, out_dtype, tm=1024, tn=1024):
    T, K = x.shape
    N = w.shape[1]
    tm = min(tm, T)
    tn = min(tn, N)
    return pl.pallas_call(
        _proj_kernel,
        grid=(T // tm, N // tn),
        in_specs=[pl.BlockSpec((tm, K), lambda i, j: (i, 0)),
                  pl.BlockSpec((K, tn), lambda i, j: (0, j))],
        out_specs=pl.BlockSpec((tm, tn), lambda i, j: (i, j)),
        out_shape=jax.ShapeDtypeStruct((T, N), out_dtype),
        compiler_params=_params("parallel", "arbitrary"),
        name="proj",
    )(x, w)


def _glu_proj_kernel(x_ref, wv_ref, wg_ref, bv_ref, bg_ref, o_ref):
    x = x_ref[...].astype(BF16)
    val = _dot(x, wv_ref[...]) + bv_ref[...]
    gate = _dot(x, wg_ref[...]) + bg_ref[...]
    o_ref[...] = (val * jax.nn.sigmoid(gate)).astype(o_ref.dtype)


def _glu_proj(x, w, b, out_dtype, tm=1024, tn=512):
    T, K = x.shape
    N = w.shape[1] // 2
    tm = min(tm, T)
    nj = N // tn
    b2 = b.reshape(1, 2 * N)
    return pl.pallas_call(
        _glu_proj_kernel,
        grid=(T // tm, nj),
        in_specs=[pl.BlockSpec((tm, K), lambda i, j: (i, 0)),
                  pl.BlockSpec((K, tn), lambda i, j: (0, j)),
                  pl.BlockSpec((K, tn), lambda i, j: (0, nj + j)),
                  pl.BlockSpec((1, tn), lambda i, j: (0, j)),
                  pl.BlockSpec((1, tn), lambda i, j: (0, nj + j))],
        out_specs=pl.BlockSpec((tm, tn), lambda i, j: (i, j)),
        out_shape=jax.ShapeDtypeStruct((T, N), out_dtype),
        compiler_params=_params("parallel", "arbitrary"),
        name="glu_proj",
    )(x, w, w, b2, b2)


def _out_ln_kernel(a_ref, w_ref, b_ref, res_ref, g_ref, beta_ref, o_ref, obf_ref):
    mix = _dot(a_ref[...], w_ref[...]) + b_ref[...]
    y = _layer_norm(ALPHA * res_ref[...] + mix, g_ref[...], beta_ref[...])
    o_ref[...] = y
    obf_ref[...] = y.astype(BF16)


def _out_ln_route_kernel(a_ref, w_ref, b_ref, res_ref, g_ref, beta_ref, wr_ref, br_ref,
                         o_ref, obf_ref, route_ref):
    mix = _dot(a_ref[...], w_ref[...]) + b_ref[...]
    y = _layer_norm(ALPHA * res_ref[...] + mix, g_ref[...], beta_ref[...])
    o_ref[...] = y
    obf_ref[...] = y.astype(BF16)
    l1 = jnp.sum(y * wr_ref[0:1, :], axis=-1, keepdims=True) + br_ref[0]
    i1 = jnp.zeros_like(l1)
    l2 = jnp.full_like(l1, -jnp.inf)
    i2 = jnp.zeros_like(l1)
    for e in range(1, N_EXPERTS):
        v = jnp.sum(y * wr_ref[e:e + 1, :], axis=-1, keepdims=True) + br_ref[e]
        gt1 = v > l1
        gt2 = v > l2
        l2 = jnp.where(gt1, l1, jnp.where(gt2, v, l2))
        i2 = jnp.where(gt1, i1, jnp.where(gt2, float(e), i2))
        l1 = jnp.where(gt1, v, l1)
        i1 = jnp.where(gt1, float(e), i1)
    e2 = jnp.exp(l2 - l1)
    w1 = 1.0 / (1.0 + e2)
    w2 = e2 / (1.0 + e2)
    lane = lax.broadcasted_iota(jnp.int32, route_ref.shape, 1)
    route_ref[...] = jnp.where(lane == 0, i1, jnp.where(lane == 1, i2, jnp.where(
        lane == 2, w1, jnp.where(lane == 3, w2, 0.0))))


def _out_ln(a, w, b, res, g, beta, router=None, tm=512):
    T, K = a.shape
    D = w.shape[1]
    row = lambda i: (i, 0)
    fixed = lambda i: (0, 0)
    in_specs = [pl.BlockSpec((tm, K), row), pl.BlockSpec((K, D), fixed), pl.BlockSpec((1, D), fixed),
                pl.BlockSpec((tm, D), row), pl.BlockSpec((1, D), fixed), pl.BlockSpec((1, D), fixed)]
    out_specs = [pl.BlockSpec((tm, D), row), pl.BlockSpec((tm, D), row)]
    out_shape = [jax.ShapeDtypeStruct((T, D), F32), jax.ShapeDtypeStruct((T, D), BF16)]
    args = [a, w, b.reshape(1, D), res, g.reshape(1, D), beta.reshape(1, D)]
    if router is None:
        body = _out_ln_kernel
    else:
        body = _out_ln_route_kernel
        w_r, b_r = router
        in_specs += [pl.BlockSpec((N_EXPERTS, D), fixed), pl.BlockSpec(memory_space=pltpu.SMEM)]
        out_specs.append(pl.BlockSpec((tm, LANES), row))
        out_shape.append(jax.ShapeDtypeStruct((T, LANES), F32))
        args += [w_r.T, b_r]
    return pl.pallas_call(
        body, grid=(T // tm,), in_specs=in_specs, out_specs=out_specs, out_shape=out_shape,
        compiler_params=_params("parallel"), name="out_ln",
    )(*args)


def _ln_ple(x1, ff, g, beta, wpg, p, wp):
    y = _layer_norm(ALPHA * x1 + ff, g, beta)
    gate = jax.nn.sigmoid(_dot(y.astype(BF16), wpg))
    return y + gate * _dot(p.astype(BF16), wp)


def _ffn_tail_kernel(x_ref, xbf_ref, wgu_ref, wd_ref, g_ref, beta_ref, wpg_ref, p_ref, wp_ref,
                     o_ref, obf_ref, *, tf):
    xb = xbf_ref[...]
    ff = None
    for c in range(D_FF // tf):
        gate = _dot(xb, wgu_ref[:, c * tf:(c + 1) * tf])
        up = _dot(xb, wgu_ref[:, D_FF + c * tf:D_FF + (c + 1) * tf])
        h = (jax.nn.silu(gate) * up).astype(BF16)
        part = _dot(h, wd_ref[c * tf:(c + 1) * tf, :])
        ff = part if ff is None else ff + part
    out = _ln_ple(x_ref[...], ff, g_ref[...], beta_ref[...], wpg_ref[...], p_ref[...], wp_ref[...])
    o_ref[...] = out
    obf_ref[...] = out.astype(BF16)


def _resident(shape):
    return pl.BlockSpec(shape, lambda i: (0,) * len(shape), pipeline_mode=pl.Buffered(1))


def _ffn_tail(x1, x1_bf, w_gu, w_down, g, beta, wpg, p, wp, tm=512, tf=256):
    T, D = x1.shape
    row = lambda i: (i, 0)
    return pl.pallas_call(
        functools.partial(_ffn_tail_kernel, tf=tf),
        grid=(T // tm,),
        in_specs=[pl.BlockSpec((tm, D), row), pl.BlockSpec((tm, D), row),
                  _resident((D, 2 * D_FF)), _resident((D_FF, D)),
                  _resident((1, D)), _resident((1, D)), _resident((D, D)),
                  pl.BlockSpec((tm, PLE_DIM), row), _resident((PLE_DIM, D))],
        out_specs=[pl.BlockSpec((tm, D), row), pl.BlockSpec((tm, D), row)],
        out_shape=[jax.ShapeDtypeStruct((T, D), F32), jax.ShapeDtypeStruct((T, D), BF16)],
        compiler_params=_params("parallel"),
        name="ffn_tail",
    )(x1, x1_bf, w_gu, w_down, g.reshape(1, D), beta.reshape(1, D), wpg, p, wp)


def _moe_tail_kernel(x_ref, y0_ref, y1_ref, route_ref, g_ref, beta_ref, wpg_ref, p_ref, wp_ref, o_ref, obf_ref):
    route = route_ref[...]
    lane = lax.broadcasted_iota(jnp.int32, route.shape, 1)
    ff = (y0_ref[...] * _lane_pick(route, lane, TOP_K) + y1_ref[...] * _lane_pick(route, lane, TOP_K + 1))
    out = _ln_ple(x_ref[...], ff, g_ref[...], beta_ref[...], wpg_ref[...], p_ref[...], wp_ref[...])
    o_ref[...] = out
    obf_ref[...] = out.astype(BF16)


def _moe_tail(x1, y0, y1, route, g, beta, wpg, p, wp, tm=512):
    T, D = x1.shape
    row = lambda i: (i, 0)
    return pl.pallas_call(
        _moe_tail_kernel,
        grid=(T // tm,),
        in_specs=[pl.BlockSpec((tm, D), row), pl.BlockSpec((tm, D), row), pl.BlockSpec((tm, D), row),
                  pl.BlockSpec((tm, LANES), row), _resident((1, D)), _resident((1, D)), _resident((D, D)),
                  pl.BlockSpec((tm, PLE_DIM), row), _resident((PLE_DIM, D))],
        out_specs=[pl.BlockSpec((tm, D), row), pl.BlockSpec((tm, D), row)],
        out_shape=[jax.ShapeDtypeStruct((T, D), F32), jax.ShapeDtypeStruct((T, D), BF16)],
        compiler_params=_params("parallel"),
        name="moe_tail",
    )(x1, y0, y1, route, g.reshape(1, D), beta.reshape(1, D), wpg, p, wp)


def _fox_decay_kernel(f_ref, bf_ref, c_ref):
    x = f_ref[...] + bf_ref[...]
    c_ref[...] = _scan_rows(None, -_softplus(-x), x.shape[0])


def _fox_decay(f_logit, b_f, batch, seq):
    return pl.pallas_call(
        _fox_decay_kernel,
        grid=(batch,),
        in_specs=[pl.BlockSpec((seq, LANES), lambda b: (b, 0)), pl.BlockSpec((1, LANES), lambda b: (0, 0))],
        out_specs=pl.BlockSpec((seq, LANES), lambda b: (b, 0)),
        out_shape=jax.ShapeDtypeStruct((batch * seq, LANES), F32),
        compiler_params=_params("parallel"),
        name="fox_decay",
    )(f_logit, b_f)


FOX_SUB = 128


def _fox_attn_kernel(q_ref, k_ref, v_ref, c_ref, o_ref, ckb_ref, vt_ref, *, tq):
    g = pl.program_id(1)
    qi = pl.program_id(2)
    S = k_ref.shape[0]
    ts = min(FOX_SUB, tq)
    nsub = tq // ts

    @pl.when(qi == 0)
    def _():
        c = c_ref[...]
        lane_s = lax.broadcasted_iota(jnp.int32, (S, LANES), 1)
        for hh in range(2):
            ckb_ref[hh] = jnp.broadcast_to(_lane_pick(c, lane_s, 2 * g + hh), (S, LANES))
        vt_ref[...] = v_ref[...].astype(F32).T.astype(BF16)

    lane = lax.broadcasted_iota(jnp.int32, (ts, LANES), 1)
    chains = [(i, hh) for i in range(nsub) for hh in range(2)]
    qs = []
    for i, hh in chains:
        q = q_ref[i * ts:(i + 1) * ts, :] * jnp.asarray(FOX_HEAD_DIM ** -0.5, BF16)
        qs.append(jnp.where((lane >= FOX_HEAD_DIM) == (hh == 1), q, jnp.zeros_like(q)))

    def block(j, carry, diagonal):
        start = pl.multiple_of(j * tq, tq)
        width = [(i + 1) * ts if diagonal else tq for i in range(nsub)]
        qk = [_dot_nt(k_ref[pl.ds(start, width[i]), :], qs[c]) for c, (i, hh) in enumerate(chains)]
        stats = []
        for c, (i, hh) in enumerate(chains):
            m, l, _ = carry[c]
            s = qk[c] - ckb_ref[hh, pl.ds(start, width[i]), :]
            if diagonal:
                key = lax.broadcasted_iota(jnp.int32, s.shape, 0)
                qry = lax.broadcasted_iota(jnp.int32, s.shape, 1)
                s = jnp.where(key <= qry + i * ts, s, NEG_BIG)
            m_new = jnp.maximum(m, jnp.max(s, axis=0, keepdims=True))
            scale = jnp.exp(m - m_new)
            pr = jnp.exp(s - m_new)
            stats.append((m_new, scale * l + jnp.sum(pr, axis=0, keepdims=True), scale, pr.astype(BF16)))
        pv = [_dot(vt_ref[:, pl.ds(start, width[i])], stats[c][3]) for c, (i, hh) in enumerate(chains)]
        return tuple((stats[c][0], stats[c][1], stats[c][2] * carry[c][2] + pv[c]) for c in range(len(chains)))

    init = tuple((jnp.full((1, ts), NEG_BIG, F32), jnp.zeros((1, ts), F32), jnp.zeros((LANES, ts), F32))
                 for _ in chains)
    carry = lax.fori_loop(0, qi, lambda j, c: block(j, c, False), init)
    final = block(qi, carry, True)
    dim = lax.broadcasted_iota(jnp.int32, (LANES, ts), 0)
    for i in range(nsub):
        (_, l0, a0), (_, l1, a1) = final[2 * i], final[2 * i + 1]
        out_t = jnp.where(dim < FOX_HEAD_DIM, a0 / l0, a1 / l1)
        o_ref[i * ts:(i + 1) * ts, :] = out_t.T.astype(o_ref.dtype)


def _fox_attn(qkv, c, batch, seq, tq=512):
    tq = min(tq, seq)
    nq = seq // tq
    pairs = FOX_HEADS // 2
    return pl.pallas_call(
        functools.partial(_fox_attn_kernel, tq=tq),
        grid=(batch, pairs, nq),
        in_specs=[pl.BlockSpec((tq, LANES), lambda b, g, i: (b * nq + i, g)),
                  pl.BlockSpec((seq, LANES), lambda b, g, i: (b, pairs + g)),
                  pl.BlockSpec((seq, LANES), lambda b, g, i: (b, 2 * pairs + g)),
                  pl.BlockSpec((seq, LANES), lambda b, g, i: (b, 0))],
        out_specs=pl.BlockSpec((tq, LANES), lambda b, g, i: (b * nq + i, g)),
        out_shape=jax.ShapeDtypeStruct((batch * seq, D_MODEL), BF16),
        scratch_shapes=[pltpu.VMEM((2, seq, LANES), F32), pltpu.VMEM((LANES, seq), BF16)],
        compiler_params=_params("parallel", "parallel", "arbitrary"),
        name="fox_attn",
    )(qkv, qkv, qkv, c)


def _pad_lanes(w):
    return jnp.zeros((w.shape[0], LANES), w.dtype).at[:, :w.shape[1]].set(w)


def _fox_mixer(x, x_bf, batch, seq, w_in, b_f, w_out, ln_g, ln_b, router=None):
    D = D_MODEL
    qkv = _proj(x_bf, w_in[:, :3 * D].astype(BF16), BF16)
    f_logit = _proj(x_bf, _pad_lanes(w_in[:, 3 * D:]).astype(BF16), F32, tn=LANES)
    c = _fox_decay(f_logit, _pad_lanes(b_f.reshape(1, -1)), batch, seq)
    o = _fox_attn(qkv, c, batch, seq)
    return _out_ln(o, w_out.astype(BF16), jnp.zeros((D,), F32), x, ln_g, ln_b, router)


PAD_ROWS = 8


def _causal_conv4(src, cw_ref, pad_ref):
    S = src.shape[0]
    pad_ref[0:PAD_ROWS, :] = jnp.zeros((PAD_ROWS, src.shape[1]), F32)
    pad_ref[PAD_ROWS:PAD_ROWS + S, :] = src
    out = None
    for k in range(4):
        off = PAD_ROWS - 3 + k
        term = cw_ref[k:k + 1, :] * pad_ref[off:off + S, :]
        out = term if out is None else out + term
    return out


def _lru_core_kernel(gate_ref, rec_ref, cw_ref, cb_ref, wa_ref, ba_ref, wx_ref, bx_ref, lam_ref, y_ref, pad_ref):
    S = rec_ref.shape[0]
    u = _causal_conv4(rec_ref[...].astype(F32), cw_ref, pad_ref) + cb_ref[...]
    ub = u.astype(BF16)
    r = jax.nn.sigmoid(_dot(ub, wa_ref[...]) + ba_ref[...])
    i = jax.nn.sigmoid(_dot(ub, wx_ref[...]) + bx_ref[...])
    log_a = -LRU_C * r * _softplus(-lam_ref[...])
    a = jnp.exp(log_a)
    th = jnp.tanh(log_a)
    b = jnp.sqrt(-2.0 * th / (1.0 - th)) * (i * u)
    h = _scan_rows(a, b, S)
    y_ref[...] = (jax.nn.gelu(gate_ref[...].astype(F32)) * h).astype(y_ref.dtype)


def _lru_core(gr, conv_w, conv_b, w_a, b_a, w_x, b_x, lam, batch, seq):
    nb = LRU_BLOCKS
    vec = lambda v: v.reshape(1, LRU_WIDTH)
    lane_blk = pl.BlockSpec((1, LANES), lambda b, n: (0, n))
    mat_blk = pl.BlockSpec((None, LANES, LANES), lambda b, n: (n, 0, 0))
    return pl.pallas_call(
        _lru_core_kernel,
        grid=(batch, nb),
        in_specs=[pl.BlockSpec((seq, LANES), lambda b, n: (b, n)),
                  pl.BlockSpec((seq, LANES), lambda b, n: (b, nb + n)),
                  pl.BlockSpec((LRU_CONV, LANES), lambda b, n: (0, n)),
                  lane_blk, mat_blk, lane_blk, mat_blk, lane_blk, lane_blk],
        out_specs=pl.BlockSpec((seq, LANES), lambda b, n: (b, n)),
        out_shape=jax.ShapeDtypeStruct((batch * seq, LRU_WIDTH), BF16),
        scratch_shapes=[pltpu.VMEM((seq + PAD_ROWS, LANES), F32)],
        compiler_params=_params("parallel", "parallel"),
        name="lru_core",
    )(gr, gr, conv_w, vec(conv_b), w_a.astype(BF16), vec(b_a), w_x.astype(BF16), vec(b_x), vec(lam))


def _lru_mixer(x, x_bf, batch, seq, w_in, conv_w, conv_b, w_a, b_a, w_x, b_x, lam, w_out, ln_g, ln_b,
               router=None):
    gr = _proj(x_bf, w_in.astype(BF16), BF16, tn=LRU_WIDTH)
    y = _lru_core(gr, conv_w, conv_b, w_a, b_a, w_x, b_x, lam, batch, seq)
    return _out_ln(y, w_out.astype(BF16), jnp.zeros((D_MODEL,), F32), x, ln_g, ln_b, router)


CONF_HALO = 32


def _conf_tail_kernel(h_ref, halo_ref, dw_ref, dwb_ref, cg_ref, cb_ref, w_ref, b_ref, res_ref, g_ref, beta_ref,
                      o_ref, obf_ref, buf_ref, *, tt):
    first = pl.program_id(1) == 0
    halo = halo_ref[...].astype(F32)
    buf_ref[0:CONF_HALO, :] = jnp.where(first, 0.0, halo)
    buf_ref[CONF_HALO:CONF_HALO + tt, :] = h_ref[...].astype(F32)
    first = CONF_HALO - (CONF_KERNEL - 1)
    conv = None
    for s in range(SUBLANES):
        offs = [o for o in range(first, first + CONF_KERNEL) if o % SUBLANES == s]
        span = offs[-1] - s + tt
        slab = buf_ref[s:s + span, :]
        part = None
        for o in offs:
            term = dw_ref[o - first:o - first + 1, :] * slab[o - s:o - s + tt]
            part = term if part is None else part + term
        conv = part if conv is None else conv + part
    hs = jax.nn.silu(_layer_norm(conv + dwb_ref[...], cg_ref[...], cb_ref[...]))
    mix = _dot(hs.astype(BF16), w_ref[...]) + b_ref[...]
    y = _layer_norm(ALPHA * res_ref[...] + mix, g_ref[...], beta_ref[...])
    o_ref[...] = y
    obf_ref[...] = y.astype(BF16)


def _conf_tail(h, dw_w, dw_b, cg, cb, w_out, b_out, res, ln_g, ln_b, batch, seq, tt=256):
    D = D_MODEL
    nt = seq // tt
    per = tt // CONF_HALO
    row = lambda b, i: (b * nt + i, 0)
    fixed = lambda b, i: (0, 0)
    vec = lambda v: v.reshape(1, D)
    return pl.pallas_call(
        functools.partial(_conf_tail_kernel, tt=tt),
        grid=(batch, nt),
        in_specs=[pl.BlockSpec((tt, D), row),
                  pl.BlockSpec((CONF_HALO, D), lambda b, i: (jnp.maximum((b * nt + i) * per - 1, 0), 0)),
                  pl.BlockSpec((CONF_KERNEL, D), fixed), pl.BlockSpec((1, D), fixed),
                  pl.BlockSpec((1, D), fixed), pl.BlockSpec((1, D), fixed),
                  pl.BlockSpec((D, D), fixed), pl.BlockSpec((1, D), fixed),
                  pl.BlockSpec((tt, D), row), pl.BlockSpec((1, D), fixed), pl.BlockSpec((1, D), fixed)],
        out_specs=[pl.BlockSpec((tt, D), row), pl.BlockSpec((tt, D), row)],
        out_shape=[jax.ShapeDtypeStruct((batch * seq, D), F32), jax.ShapeDtypeStruct((batch * seq, D), BF16)],
        scratch_shapes=[pltpu.VMEM((CONF_HALO + tt, D), F32)],
        compiler_params=_params("parallel", "arbitrary"),
        name="conf_tail",
    )(h, h, dw_w, vec(dw_b), vec(cg), vec(cb), w_out, vec(b_out), res, vec(ln_g), vec(ln_b))


def _conf_mixer(x, x_bf, batch, seq, w_in, b_in, dw_w, dw_b, cg, cb, w_out, b_out, ln_g, ln_b):
    h = _glu_proj(x_bf, w_in.astype(BF16), b_in, BF16)
    return _conf_tail(h, dw_w, dw_b, cg, cb, w_out.astype(BF16), b_out, x, ln_g, ln_b, batch, seq)


def _gdn_core_kernel(q_ref, k_ref, v_ref, z_ref, ba_ref, cwq_ref, cwk_ref, cwv_ref, alog_ref, dtb_ref, ng_ref,
                     pm_ref, o_ref,
                     pad_ref, qs_ref, ks_ref, kb_ref, vb_ref, kbe_ref, gcs_ref,
                     u_ref, w_ref, kend_ref, qg_ref, attn_ref, egl_ref, *, hb, tile):
    S = q_ref.shape[0]
    C = tile
    Dh = GDN_HEAD_DIM
    n_chunks = S // C
    grp = pl.program_id(1)
    lane = lax.broadcasted_iota(jnp.int32, (S, LANES), 1)

    ba = ba_ref[...]
    beta_all = jax.nn.sigmoid(ba)
    g_all = -jnp.exp(alog_ref[...]) * _softplus(ba + dtb_ref[...])
    gc_all = _scan_rows(None, g_all, C)

    def l2n(t):
        return t * lax.rsqrt(jnp.sum(t * t, axis=-1, keepdims=True) + 1e-6)

    for hh in range(hb):
        head = grp * hb + hh
        hs = slice(hh * Dh, (hh + 1) * Dh)
        beta = _lane_pick(beta_all, lane, head)
        gc = _lane_pick(gc_all, lane, GDN_HEADS + head)
        q = jax.nn.silu(_causal_conv4(q_ref[:, hs].astype(F32), cwq_ref.at[:, hs], pad_ref))
        k = jax.nn.silu(_causal_conv4(k_ref[:, hs].astype(F32), cwk_ref.at[:, hs], pad_ref))
        v = jax.nn.silu(_causal_conv4(v_ref[:, hs].astype(F32), cwv_ref.at[:, hs], pad_ref))
        q = l2n(q) * (Dh ** -0.5)
        k = l2n(k)
        eg = jnp.exp(gc)
        kb = k * beta
        qs_ref[hh] = q.astype(BF16)
        ks_ref[hh] = k.astype(BF16)
        kb_ref[hh] = kb.astype(BF16)
        vb_ref[hh] = (v * beta).astype(BF16)
        kbe_ref[hh] = (kb * eg).astype(BF16)
        qg_ref[hh] = (q * eg).astype(BF16)
        gcs_ref[hh] = jnp.broadcast_to(gc, (S, LANES))

    row = lax.broadcasted_iota(jnp.int32, (C, C), 0)
    col = lax.broadcasted_iota(jnp.int32, (C, C), 1)
    lane_c = lax.broadcasted_iota(jnp.int32, (C, LANES), 1)
    n_levels = pm_ref.shape[0]
    chunks_per_step = 2 if n_chunks % 2 == 0 else 1

    def split_features(gcl):
        hi = gcl.astype(BF16).astype(F32)
        r1 = gcl - hi
        mid = r1.astype(BF16).astype(F32)
        lo = r1 - mid
        fa = jnp.where(lane_c == 0, hi, jnp.where(lane_c == 1, mid, jnp.where(
            lane_c == 2, lo, jnp.where(lane_c < 6, 1.0, 0.0))))
        fb = jnp.where(lane_c < 3, 1.0, jnp.where(lane_c == 3, -hi, jnp.where(
            lane_c == 4, -mid, jnp.where(lane_c == 5, -lo, 0.0))))
        return fa.astype(BF16), fb.astype(BF16)

    def prep(n, _):
        chains = [(hh, n * chunks_per_step + cc) for cc in range(chunks_per_step) for hh in range(hb)]
        sls = [pl.ds(pl.multiple_of(cn * C, C), C) for _, cn in chains]
        heads = [hh for hh, _ in chains]
        ch = range(len(chains))
        kc = [ks_ref[heads[i], sls[i], :] for i in ch]
        gcl = [gcs_ref[heads[i], sls[i], :] for i in ch]
        feats = [split_features(gcl[i]) for i in ch]
        gdiff = [_dot_nt(*feats[i]) for i in ch]
        kk = [_dot_nt(kb_ref[heads[i], sls[i], :], kc[i]) for i in ch]
        qk = [_dot_nt(qs_ref[heads[i], sls[i], :], kc[i]) for i in ch]
        tril = row >= col
        decay = [jnp.where(tril, jnp.exp(jnp.where(tril, gdiff[i], 0.0)), 0.0) for i in ch]
        lower = [(kk[i] * decay[i]).astype(BF16) for i in ch]
        for i in ch:
            attn_ref[heads[i], sls[i], :] = (qk[i] * decay[i]).astype(BF16)
        eye = jnp.where(row == col, 1.0, 0.0)
        inv = [eye - (lower[i] * pm_ref[0]).astype(F32) for i in ch]
        for lvl in range(1, n_levels):
            inv_b = [inv[i].astype(BF16) for i in ch]
            part = [_dot(lower[i] * pm_ref[lvl], inv_b[i]).astype(BF16) for i in ch]
            inv = [inv[i] - _dot(inv_b[i], part[i]) for i in ch]
        inv_b = [inv[i].astype(BF16) for i in ch]
        u = [_dot(inv_b[i], vb_ref[heads[i], sls[i], :]) for i in ch]
        w = [_dot(inv_b[i], kbe_ref[heads[i], sls[i], :]) for i in ch]
        for i in ch:
            hh, cn = chains[i]
            u_ref[hh, sls[i], :] = u[i]
            w_ref[hh, sls[i], :] = w[i].astype(BF16)
            gl = gcl[i][C - 1:C, :]
            kend_ref[hh, sls[i], :] = (kc[i].astype(F32) * jnp.exp(gl - gcl[i])).astype(BF16)
            egl_ref[hh, pl.ds(pl.multiple_of(cn * 8, 8), 8), :] = jnp.broadcast_to(jnp.exp(gl), (8, LANES))
        return 0

    lax.fori_loop(0, n_chunks // chunks_per_step, prep, 0)

    ng = ng_ref[...]

    def step(n, states):
        sl = pl.ds(pl.multiple_of(n * C, C), C)
        hr = range(hb)
        st_b = [states[hh].astype(BF16) for hh in hr]
        w_st = [_dot(w_ref[hh, sl, :], st_b[hh]) for hh in hr]
        q_st = [_dot(qg_ref[hh, sl, :], st_b[hh]) for hh in hr]
        v_new_b = [(u_ref[hh, sl, :] - w_st[hh]).astype(BF16) for hh in hr]
        intra = [_dot(attn_ref[hh, sl, :], v_new_b[hh]) for hh in hr]
        grow = [_dot_tn(kend_ref[hh, sl, :], v_new_b[hh]) for hh in hr]
        new = []
        for hh in hr:
            egl = egl_ref[hh, pl.ds(pl.multiple_of(n * 8, 8), 8), :][0:1, 0:1]
            new.append(states[hh] * egl + grow[hh])
            o = q_st[hh] + intra[hh]
            o = o * lax.rsqrt(jnp.mean(o * o, axis=-1, keepdims=True) + RMS_EPS) * ng
            zc = z_ref[sl, hh * Dh:(hh + 1) * Dh].astype(F32)
            o_ref[sl, hh * Dh:(hh + 1) * Dh] = (o * jax.nn.silu(zc)).astype(o_ref.dtype)
        return tuple(new)

    lax.fori_loop(0, n_chunks, step, tuple(jnp.zeros((Dh, Dh), F32) for _ in range(hb)))


def _pair_masks(tile):
    r = jnp.arange(tile)[:, None]
    c = jnp.arange(tile)[None, :]
    levels = []
    m = 1
    while m < tile:
        levels.append(((r // m) == (c // m) + 1) & ((r // (2 * m)) == (c // (2 * m))))
        m *= 2
    return jnp.stack(levels).astype(BF16)


def _gdn_core(qkvz, ba, conv_w, a_log, dt_bias, norm_g, batch, seq, hb=2, tile=GDN_TILE):
    H, Dh = GDN_HEADS, GDN_HEAD_DIM
    W = hb * Dh
    ng_ = H // hb
    tile = min(tile, seq)
    masks = _pair_masks(tile)
    blk = lambda part: pl.BlockSpec((seq, W), lambda b, g: (b, part * ng_ + g))
    cw = lambda part: pl.BlockSpec((GDN_CONV, W), lambda b, g: (0, part * ng_ + g))
    fixed = pl.BlockSpec((1, LANES), lambda b, g: (0, 0))
    pad = lambda v: jnp.zeros((1, LANES), F32).at[0, H:2 * H].set(v)
    return pl.pallas_call(
        functools.partial(_gdn_core_kernel, hb=hb, tile=tile),
        grid=(batch, ng_),
        in_specs=[blk(0), blk(1), blk(2), blk(3), pl.BlockSpec((seq, LANES), lambda b, g: (b, 0)),
                  cw(0), cw(1), cw(2), fixed, fixed, fixed,
                  pl.BlockSpec(masks.shape, lambda b, g: (0, 0, 0))],
        out_specs=pl.BlockSpec((seq, W), lambda b, g: (b, g)),
        out_shape=jax.ShapeDtypeStruct((batch * seq, H * Dh), BF16),
        scratch_shapes=[pltpu.VMEM((seq + PAD_ROWS, Dh), F32)]
                       + [pltpu.VMEM((hb, seq, Dh), BF16)] * 5
                       + [pltpu.VMEM((hb, seq, LANES), F32),
                          pltpu.VMEM((hb, seq, Dh), F32),
                          pltpu.VMEM((hb, seq, Dh), BF16),
                          pltpu.VMEM((hb, seq, Dh), BF16),
                          pltpu.VMEM((hb, seq, Dh), BF16),
                          pltpu.VMEM((hb, seq, tile), BF16),
                          pltpu.VMEM((hb, seq // tile * 8, LANES), F32)],
        compiler_params=_params("parallel", "parallel"),
        name="gdn_core",
    )(qkvz, qkvz, qkvz, qkvz, ba, conv_w, conv_w, conv_w, pad(a_log), pad(dt_bias), norm_g.reshape(1, Dh), masks)


def _gdn_mixer(x, x_bf, batch, seq, w_in, conv_w, a_log, dt_bias, norm_g, w_out, ln_g, ln_b, router=None):
    HD = GDN_HEADS * GDN_HEAD_DIM
    qkvz = _proj(x_bf, w_in[:, :4 * HD].astype(BF16), BF16)
    ba = _proj(x_bf, _pad_lanes(w_in[:, 4 * HD:]).astype(BF16), F32, tn=LANES)
    o = _gdn_core(qkvz, ba, conv_w, a_log, dt_bias, norm_g, batch, seq)
    return _out_ln(o, w_out.astype(BF16), jnp.zeros((D_MODEL,), F32), x, ln_g, ln_b, router)


MOE_TILE = 1024


def _moe_expert_kernel(te_ref, tv_ref, x_ref, wg_ref, wu_ref, wd_ref, o_ref, xb_ref):
    i = pl.program_id(0)
    j = pl.program_id(1)
    valid = tv_ref[i] == 1
    half = MOE_TILE // 2
    halves = [pl.ds(0, half), pl.ds(half, half)]

    @pl.when(j == 0)
    def _():
        xb_ref[...] = x_ref[...].astype(BF16)
        o_ref[...] = jnp.zeros_like(o_ref)

    @pl.when(valid)
    def _():
        wg = wg_ref[...].astype(BF16)
        wu = wu_ref[...].astype(BF16)
        wd = wd_ref[...].astype(BF16)
        gate = [_dot(xb_ref[h, :], wg) for h in halves]
        up = [_dot(xb_ref[h, :], wu) for h in halves]
        hid = [(jax.nn.silu(gate[k]) * up[k]).astype(BF16) for k in range(2)]
        for k, h in enumerate(halves):
            o_ref[h, :] += _dot(hid[k], wd)


def _moe_experts(xs, tile_e, tile_valid, w_gu, w_down, layer, tf=512):
    n_rows, D = xs.shape
    n_tiles = n_rows // MOE_TILE
    nc = D_EXPERT // tf

    def chunk(j, tv, i):
        return jnp.where(tv[i] == 1, j, nc - 1)

    grid_spec = pltpu.PrefetchScalarGridSpec(
        num_scalar_prefetch=2,
        grid=(n_tiles, nc),
        in_specs=[pl.BlockSpec((MOE_TILE, D), lambda i, j, te, tv: (i, 0)),
                  pl.BlockSpec((None, None, D, tf), lambda i, j, te, tv: (layer, te[i], 0, chunk(j, tv, i))),
                  pl.BlockSpec((None, None, D, tf), lambda i, j, te, tv: (layer, te[i], 0, nc + chunk(j, tv, i))),
                  pl.BlockSpec((None, None, tf, D), lambda i, j, te, tv: (layer, te[i], chunk(j, tv, i), 0))],
        out_specs=pl.BlockSpec((MOE_TILE, D), lambda i, j, te, tv: (i, 0)),
        scratch_shapes=[pltpu.VMEM((MOE_TILE, D), BF16)],
    )
    return pl.pallas_call(
        _moe_expert_kernel,
        grid_spec=grid_spec,
        out_shape=jax.ShapeDtypeStruct((n_rows, D), F32),
        compiler_params=_params("arbitrary", "arbitrary"),
        name="moe_experts",
    )(tile_e, tile_valid, xs, w_gu, w_gu, w_down)


def _moe_ffn(x1, route, w_gu, w_down, layer, g, beta, wpg, p, wp):
    T, D = x1.shape
    n_pairs = T * TOP_K
    n_tiles = n_pairs // MOE_TILE + N_EXPERTS
    n_rows = n_tiles * MOE_TILE
    flat_e = route[:, :TOP_K].astype(jnp.int32).reshape(-1)
    onehot = (flat_e[:, None] == jnp.arange(N_EXPERTS, dtype=jnp.int32)[None, :]).astype(jnp.int32)
    csum = jnp.cumsum(onehot, axis=0)
    counts = csum[-1]
    rank = jnp.sum((csum - 1) * onehot, axis=1)
    padded = ((counts + MOE_TILE - 1) // MOE_TILE) * MOE_TILE
    pad_end = jnp.cumsum(padded)
    pad_start = pad_end - padded
    dest = pad_start[flat_e] + rank
    row_tok = jnp.zeros((n_rows,), jnp.int32).at[dest].set(jnp.arange(n_pairs, dtype=jnp.int32) // TOP_K)
    tile_start = jnp.arange(n_tiles, dtype=jnp.int32) * MOE_TILE
    tile_valid = tile_start < pad_end[-1]
    tile_e = jnp.minimum(jnp.searchsorted(pad_end, tile_start, side="right"), N_EXPERTS - 1).astype(jnp.int32)
    tile_e = jnp.where(tile_valid, tile_e, jnp.max(jnp.where(tile_valid, tile_e, 0)))
    xs = x1[row_tok]
    y = _moe_experts(xs, tile_e, tile_valid.astype(jnp.int32), w_gu, w_down, layer)
    dest2 = dest.reshape(T, TOP_K)
    return _moe_tail(x1, y[dest2[:, 0]], y[dest2[:, 1]], route, g, beta, wpg, p, wp)


def kernel(x, p, ln_mix_g, ln_mix_b, ln_ffn_g, ln_ffn_b, ple_w, ple_gate_w, fox_w_in, fox_b_f, fox_w_out, lru_w_in, lru_conv_w, lru_conv_b, lru_w_a, lru_b_a, lru_w_x, lru_b_x, lru_lambda, lru_w_out, cv_w_in, cv_b_in, cv_dw_w, cv_dw_b, cv_ln_g, cv_ln_b, cv_w_out, cv_b_out, gdn_w_in, gdn_conv_w, gdn_a_log, gdn_dt_bias, gdn_norm_g, gdn_w_out, ffn_w_gu, ffn_w_down, moe_w_router, moe_b_router, moe_w_gu, moe_w_down):
    B, S, D = x.shape
    T = B * S
    xf = x.reshape(T, D)
    xb = xf
    for i in range(DEPTH):
        m, j = i % 4, i // 4
        router = (moe_w_router[i // 2], moe_b_router[i // 2]) if i % 2 == 1 else None
        ln = (ln_mix_g[i], ln_mix_b[i])
        if m == 0:
            mixed = _fox_mixer(xf, xb, B, S, fox_w_in[j], fox_b_f[j], fox_w_out[j], *ln, router)
        elif m == 1:
            mixed = _lru_mixer(xf, xb, B, S, lru_w_in[j], lru_conv_w[j], lru_conv_b[j], lru_w_a[j], lru_b_a[j],
                               lru_w_x[j], lru_b_x[j], lru_lambda[j], lru_w_out[j], *ln, router)
        elif m == 2:
            mixed = _conf_mixer(xf, xb, B, S, cv_w_in[j], cv_b_in[j], cv_dw_w[j], cv_dw_b[j], cv_ln_g[j],
                                cv_ln_b[j], cv_w_out[j], cv_b_out[j], *ln)
        else:
            mixed = _gdn_mixer(xf, xb, B, S, gdn_w_in[j], gdn_conv_w[j], gdn_a_log[j], gdn_dt_bias[j],
                               gdn_norm_g[j], gdn_w_out[j], *ln, router)
        tail = (ln_ffn_g[i], ln_ffn_b[i], ple_gate_w[i].astype(BF16), p[i].reshape(T, PLE_DIM),
                ple_w[i].astype(BF16))
        if i % 2 == 0:
            x1, x1b = mixed
            xf, xb = _ffn_tail(x1, x1b, ffn_w_gu[i // 2].astype(BF16), ffn_w_down[i // 2].astype(BF16), *tail)
        else:
            x1, x1b, route = mixed
            xf, xb = _moe_ffn(x1, route, moe_w_gu, moe_w_down, i // 2, *tail)
    return xf.reshape(B, S, D)
```

```python
import functools
import math

import jax
import jax.numpy as jnp
from jax import lax
from jax.experimental import pallas as pl
from jax.experimental.pallas import tpu as pltpu

F32 = jnp.float32
BF16 = jnp.bfloat16

D_MODEL = 1024
DEPTH = 4
PLE_DIM = 256
ALPHA = (2 * DEPTH) ** 0.25
LN_EPS = 1e-5
RMS_EPS = 1e-6
FOX_HEADS = 16
FOX_HEAD_DIM = 64
LRU_WIDTH = 1280
LRU_BLOCKS = 10
LRU_CONV = 4
LRU_C = 8.0
CONF_KERNEL = 31
GDN_HEADS = 8
GDN_HEAD_DIM = 128
GDN_CONV = 4
GDN_TILE = 256
D_FF = 2816
N_EXPERTS = 8
TOP_K = 2
D_EXPERT = 3584

LANES = 128
SUBLANES = 8
VMEM_LIMIT = 56 * 1024 * 1024
NEG_BIG = -1e30
BATCH_GROUPS = 2


def _params(*sem):
    return pltpu.CompilerParams(dimension_semantics=sem, vmem_limit_bytes=VMEM_LIMIT)


def _dot(a, b):
    return jnp.dot(a, b, preferred_element_type=F32)


def _dot_nt(a, b):
    return lax.dot_general(a, b, (((1,), (1,)), ((), ())), preferred_element_type=F32)


def _dot_tn(a, b):
    return lax.dot_general(a, b, (((0,), (0,)), ((), ())), preferred_element_type=F32)


def _layer_norm(y, g, b):
    mu = jnp.mean(y, axis=-1, keepdims=True)
    d = y - mu
    var = jnp.mean(d * d, axis=-1, keepdims=True)
    return d * lax.rsqrt(var + LN_EPS) * g + b


def _softplus(x):
    return jnp.maximum(x, 0.0) + jnp.log1p(jnp.exp(-jnp.abs(x)))


def _lane_pick(x, lane_idx, idx):
    return jnp.sum(jnp.where(lane_idx == idx, x, 0.0), axis=-1, keepdims=True)


def _scan_rows(a, b, period):
    rows = b.shape[0]
    t = lax.broadcasted_iota(jnp.int32, b.shape, 0)
    if period < rows:
        t = t & (period - 1)
    d = 1
    while d < period:
        if period == rows and d % SUBLANES == 0:
            if a is None:
                b = jnp.concatenate([b[:d], b[d:] + b[:-d]], axis=0)
            else:
                b = jnp.concatenate([b[:d], b[d:] + a[d:] * b[:-d]], axis=0)
                a = jnp.concatenate([a[:d], a[d:] * a[:-d]], axis=0)
            d *= 2
            continue
        ok = t >= d
        b_sh = jnp.where(ok, pltpu.roll(b, d, axis=0), 0.0)
        if a is None:
            b = b + b_sh
        else:
            b = b + a * b_sh
            a = a * jnp.where(ok, pltpu.roll(a, d, axis=0), 1.0)
        d *= 2
    return b


def _proj_kernel(x_ref, w_ref, o_ref):
    o_ref[...] = _dot(x_ref[...].astype(BF16), w_ref[...]).astype(o_ref.dtype)


def _proj(x, w, out_dtype, tm=1024, tn=1024):
    T, K = x.shape
    N = w.shape[1]
    tm = min(tm, T)
    tn = min(tn, N)
    return pl.pallas_call(
        _proj_kernel,
        grid=(T // tm, N // tn),
        in_specs=[pl.BlockSpec((tm, K), lambda i, j: (i, 0)),
                  pl.BlockSpec((K, tn), lambda i, j: (0, j))],
        out_specs=pl.BlockSpec((tm, tn), lambda i, j: (i, j)),
        out_shape=jax.ShapeDtypeStruct((T, N), out_dtype),
        compiler_params=_params("parallel", "arbitrary"),
        name="proj",
    )(x, w)


def _glu_proj_kernel(x_ref, wv_ref, wg_ref, bv_ref, bg_ref, o_ref):
    x = x_ref[...].astype(BF16)
    val = _dot(x, wv_ref[...]) + bv_ref[...]
    gate = _dot(x, wg_ref[...]) + bg_ref[...]
    o_ref[...] = (val * jax.nn.sigmoid(gate)).astype(o_ref.dtype)


def _glu_proj(x, w, b, out_dtype, tm=1024, tn=512):
    T, K = x.shape
    N = w.shape[1] // 2
    tm = min(tm, T)
    nj = N // tn
    b2 = b.reshape(1, 2 * N)
    return pl.pallas_call(
        _glu_proj_kernel,
        grid=(T // tm, nj),
        in_specs=[pl.BlockSpec((tm, K), lambda i, j: (i, 0)),
                  pl.BlockSpec((K, tn), lambda i, j: (0, j)),
                  pl.BlockSpec((K, tn), lambda i, j: (0, nj + j)),
                  pl.BlockSpec((1, tn), lambda i, j: (0, j)),
                  pl.BlockSpec((1, tn), lambda i, j: (0, nj + j))],
        out_specs=pl.BlockSpec((tm, tn), lambda i, j: (i, j)),
        out_shape=jax.ShapeDtypeStruct((T, N), out_dtype),
        compiler_params=_params("parallel", "arbitrary"),
        name="glu_proj",
    )(x, w, w, b2, b2)


def _out_ln_kernel(a_ref, w_ref, b_ref, res_ref, g_ref, beta_ref, o_ref, obf_ref):
    mix = _dot(a_ref[...], w_ref[...]) + b_ref[...]
    y = _layer_norm(ALPHA * res_ref[...] + mix, g_ref[...], beta_ref[...])
    o_ref[...] = y
    obf_ref[...] = y.astype(BF16)


def _out_ln_route_kernel(a_ref, w_ref, b_ref, res_ref, g_ref, beta_ref, wr_ref, br_ref,
                         o_ref, obf_ref, route_ref):
    mix = _dot(a_ref[...], w_ref[...]) + b_ref[...]
    y = _layer_norm(ALPHA * res_ref[...] + mix, g_ref[...], beta_ref[...])
    o_ref[...] = y
    obf_ref[...] = y.astype(BF16)
    l1 = jnp.sum(y * wr_ref[0:1, :], axis=-1, keepdims=True) + br_ref[0]
    i1 = jnp.zeros_like(l1)
    l2 = jnp.full_like(l1, -jnp.inf)
    i2 = jnp.zeros_like(l1)
    for e in range(1, N_EXPERTS):
        v = jnp.sum(y * wr_ref[e:e + 1, :], axis=-1, keepdims=True) + br_ref[e]
        gt1 = v > l1
        gt2 = v > l2
        l2 = jnp.where(gt1, l1, jnp.where(gt2, v, l2))
        i2 = jnp.where(gt1, i1, jnp.where(gt2, float(e), i2))
        l1 = jnp.where(gt1, v, l1)
        i1 = jnp.where(gt1, float(e), i1)
    e2 = jnp.exp(l2 - l1)
    w1 = 1.0 / (1.0 + e2)
    w2 = e2 / (1.0 + e2)
    lane = lax.broadcasted_iota(jnp.int32, route_ref.shape, 1)
    route_ref[...] = jnp.where(lane == 0, i1, jnp.where(lane == 1, i2, jnp.where(
        lane == 2, w1, jnp.where(lane == 3, w2, 0.0))))


def _out_ln(a, w, b, res, g, beta, router=None, tm=512):
    T, K = a.shape
    D = w.shape[1]
    row = lambda i: (i, 0)
    fixed = lambda i: (0, 0)
    in_specs = [pl.BlockSpec((tm, K), row), pl.BlockSpec((K, D), fixed), pl.BlockSpec((1, D), fixed),
                pl.BlockSpec((tm, D), row), pl.BlockSpec((1, D), fixed), pl.BlockSpec((1, D), fixed)]
    out_specs = [pl.BlockSpec((tm, D), row), pl.BlockSpec((tm, D), row)]
    out_shape = [jax.ShapeDtypeStruct((T, D), F32), jax.ShapeDtypeStruct((T, D), BF16)]
    args = [a, w, b.reshape(1, D), res, g.reshape(1, D), beta.reshape(1, D)]
    if router is None:
        body = _out_ln_kernel
    else:
        body = _out_ln_route_kernel
        w_r, b_r = router
        in_specs += [pl.BlockSpec((N_EXPERTS, D), fixed), pl.BlockSpec(memory_space=pltpu.SMEM)]
        out_specs.append(pl.BlockSpec((tm, LANES), row))
        out_shape.append(jax.ShapeDtypeStruct((T, LANES), F32))
        args += [w_r.T, b_r]
    return pl.pallas_call(
        body, grid=(T // tm,), in_specs=in_specs, out_specs=out_specs, out_shape=out_shape,
        compiler_params=_params("parallel"), name="out_ln",
    )(*args)


def _ln_ple(x1, ff, g, beta, wpg, p, wp):
    y = _layer_norm(ALPHA * x1 + ff, g, beta)
    gate = jax.nn.sigmoid(_dot(y.astype(BF16), wpg))
    return y + gate * _dot(p.astype(BF16), wp)


def _ffn_tail_kernel(x_ref, xbf_ref, wgu_ref, wd_ref, g_ref, beta_ref, wpg_ref, p_ref, wp_ref,
                     o_ref, obf_ref, *, tf):
    xb = xbf_ref[...]
    ff = None
    for c in range(D_FF // tf):
        gate = _dot(xb, wgu_ref[:, c * tf:(c + 1) * tf])
        up = _dot(xb, wgu_ref[:, D_FF + c * tf:D_FF + (c + 1) * tf])
        h = (jax.nn.silu(gate) * up).astype(BF16)
        part = _dot(h, wd_ref[c * tf:(c + 1) * tf, :])
        ff = part if ff is None else ff + part
    out = _ln_ple(x_ref[...], ff, g_ref[...], beta_ref[...], wpg_ref[...], p_ref[...], wp_ref[...])
    o_ref[...] = out
    obf_ref[...] = out.astype(BF16)


def _resident(shape):
    return pl.BlockSpec(shape, lambda i: (0,) * len(shape), pipeline_mode=pl.Buffered(1))


def _ffn_tail(x1, x1_bf, w_gu, w_down, g, beta, wpg, p, wp, tm=512, tf=256):
    T, D = x1.shape
    row = lambda i: (i, 0)
    return pl.pallas_call(
        functools.partial(_ffn_tail_kernel, tf=tf),
        grid=(T // tm,),
        in_specs=[pl.BlockSpec((tm, D), row), pl.BlockSpec((tm, D), row),
                  _resident((D, 2 * D_FF)), _resident((D_FF, D)),
                  _resident((1, D)), _resident((1, D)), _resident((D, D)),
                  pl.BlockSpec((tm, PLE_DIM), row), _resident((PLE_DIM, D))],
        out_specs=[pl.BlockSpec((tm, D), row), pl.BlockSpec((tm, D), row)],
        out_shape=[jax.ShapeDtypeStruct((T, D), F32), jax.ShapeDtypeStruct((T, D), BF16)],
        compiler_params=_params("parallel"),
        name="ffn_tail",
    )(x1, x1_bf, w_gu, w_down, g.reshape(1, D), beta.reshape(1, D), wpg, p, wp)


def _moe_tail_kernel(x_ref, y0_ref, y1_ref, route_ref, g_ref, beta_ref, wpg_ref, p_ref, wp_ref, o_ref, obf_ref):
    route = route_ref[...]
    lane = lax.broadcasted_iota(jnp.int32, route.shape, 1)
    ff = (y0_ref[...] * _lane_pick(route, lane, TOP_K) + y1_ref[...] * _lane_pick(route, lane, TOP_K + 1))
    out = _ln_ple(x_ref[...], ff, g_ref[...], beta_ref[...], wpg_ref[...], p_ref[...], wp_ref[...])
    o_ref[...] = out
    obf_ref[...] = out.astype(BF16)


def _moe_tail(x1, y0, y1, route, g, beta, wpg, p, wp, tm=512):
    T, D = x1.shape
    row = lambda i: (i, 0)
    return pl.pallas_call(
        _moe_tail_kernel,
        grid=(T // tm,),
        in_specs=[pl.BlockSpec((tm, D), row), pl.BlockSpec((tm, D), row), pl.BlockSpec((tm, D), row),
                  pl.BlockSpec((tm, LANES), row), _resident((1, D)), _resident((1, D)), _resident((D, D)),
                  pl.BlockSpec((tm, PLE_DIM), row), _resident((PLE_DIM, D))],
        out_specs=[pl.BlockSpec((tm, D), row), pl.BlockSpec((tm, D), row)],
        out_shape=[jax.ShapeDtypeStruct((T, D), F32), jax.ShapeDtypeStruct((T, D), BF16)],
        compiler_params=_params("parallel"),
        name="moe_tail",
    )(x1, y0, y1, route, g.reshape(1, D), beta.reshape(1, D), wpg, p, wp)


def _fox_decay_kernel(f_ref, bf_ref, c_ref):
    x = f_ref[...] + bf_ref[...]
    c_ref[...] = _scan_rows(None, -_softplus(-x), x.shape[0])


def _fox_decay(f_logit, b_f, batch, seq):
    return pl.pallas_call(
        _fox_decay_kernel,
        grid=(batch,),
        in_specs=[pl.BlockSpec((seq, LANES), lambda b: (b, 0)), pl.BlockSpec((1, LANES), lambda b: (0, 0))],
        out_specs=pl.BlockSpec((seq, LANES), lambda b: (b, 0)),
        out_shape=jax.ShapeDtypeStruct((batch * seq, LANES), F32),
        compiler_params=_params("parallel"),
        name="fox_decay",
    )(f_logit, b_f)


FOX_SUB = 128


def _fox_attn_kernel(q_ref, k_ref, v_ref, c_ref, o_ref, ckb_ref, vt_ref, *, tq):
    g = pl.program_id(1)
    qi = pl.program_id(2)
    S = k_ref.shape[0]
    ts = min(FOX_SUB, tq)
    nsub = tq // ts

    @pl.when(qi == 0)
    def _():
        c = c_ref[...]
        lane_s = lax.broadcasted_iota(jnp.int32, (S, LANES), 1)
        for hh in range(2):
            ckb_ref[hh] = jnp.broadcast_to(_lane_pick(c, lane_s, 2 * g + hh), (S, LANES))
        vt_ref[...] = v_ref[...].astype(F32).T.astype(BF16)

    lane = lax.broadcasted_iota(jnp.int32, (ts, LANES), 1)
    chains = [(i, hh) for i in range(nsub) for hh in range(2)]
    qs = []
    for i, hh in chains:
        q = q_ref[i * ts:(i + 1) * ts, :] * jnp.asarray(FOX_HEAD_DIM ** -0.5, BF16)
        qs.append(jnp.where((lane >= FOX_HEAD_DIM) == (hh == 1), q, jnp.zeros_like(q)))

    def block(j, carry, diagonal):
        start = pl.multiple_of(j * tq, tq)
        width = [(i + 1) * ts if diagonal else tq for i in range(nsub)]
        qk = [_dot_nt(k_ref[pl.ds(start, width[i]), :], qs[c]) for c, (i, hh) in enumerate(chains)]
        stats = []
        for c, (i, hh) in enumerate(chains):
            m, l, _ = carry[c]
            s = qk[c] - ckb_ref[hh, pl.ds(start, width[i]), :]
            if diagonal:
                key = lax.broadcasted_iota(jnp.int32, s.shape, 0)
                qry = lax.broadcasted_iota(jnp.int32, s.shape, 1)
                s = jnp.where(key <= qry + i * ts, s, NEG_BIG)
            m_new = jnp.maximum(m, jnp.max(s, axis=0, keepdims=True))
            scale = jnp.exp(m - m_new)
            pr = jnp.exp(s - m_new)
            stats.append((m_new, scale * l + jnp.sum(pr, axis=0, keepdims=True), scale, pr.astype(BF16)))
        pv = [_dot(vt_ref[:, pl.ds(start, width[i])], stats[c][3]) for c, (i, hh) in enumerate(chains)]
        return tuple((stats[c][0], stats[c][1], stats[c][2] * carry[c][2] + pv[c]) for c in range(len(chains)))

    init = tuple((jnp.full((1, ts), NEG_BIG, F32), jnp.zeros((1, ts), F32), jnp.zeros((LANES, ts), F32))
                 for _ in chains)
    carry = lax.fori_loop(0, qi, lambda j, c: block(j, c, False), init)
    final = block(qi, carry, True)
    dim = lax.broadcasted_iota(jnp.int32, (LANES, ts), 0)
    for i in range(nsub):
        (_, l0, a0), (_, l1, a1) = final[2 * i], final[2 * i + 1]
        out_t = jnp.where(dim < FOX_HEAD_DIM, a0 / l0, a1 / l1)
        o_ref[i * ts:(i + 1) * ts, :] = out_t.T.astype(o_ref.dtype)


def _fox_attn(qkv, c, batch, seq, tq=512):
    tq = min(tq, seq)
    nq = seq // tq
    pairs = FOX_HEADS // 2
    return pl.pallas_call(
        functools.partial(_fox_attn_kernel, tq=tq),
        grid=(batch, pairs, nq),
        in_specs=[pl.BlockSpec((tq, LANES), lambda b, g, i: (b * nq + i, g)),
                  pl.BlockSpec((seq, LANES), lambda b, g, i: (b, pairs + g)),
                  pl.BlockSpec((seq, LANES), lambda b, g, i: (b, 2 * pairs + g)),
                  pl.BlockSpec((seq, LANES), lambda b, g, i: (b, 0))],
        out_specs=pl.BlockSpec((tq, LANES), lambda b, g, i: (b * nq + i, g)),
        out_shape=jax.ShapeDtypeStruct((batch * seq, D_MODEL), BF16),
        scratch_shapes=[pltpu.VMEM((2, seq, LANES), F32), pltpu.VMEM((LANES, seq), BF16)],
        compiler_params=_params("parallel", "parallel", "arbitrary"),
        name="fox_attn",
    )(qkv, qkv, qkv, c)


def _pad_lanes(w):
    return jnp.zeros((w.shape[0], LANES), w.dtype).at[:, :w.shape[1]].set(w)


def _fox_mixer(x, x_bf, batch, seq, w_in, b_f, w_out, ln_g, ln_b, router=None):
    D = D_MODEL
    qkv = _proj(x_bf, w_in[:, :3 * D].astype(BF16), BF16)
    f_logit = _proj(x_bf, _pad_lanes(w_in[:, 3 * D:]).astype(BF16), F32, tn=LANES)
    c = _fox_decay(f_logit, _pad_lanes(b_f.reshape(1, -1)), batch, seq)
    o = _fox_attn(qkv, c, batch, seq)
    return _out_ln(o, w_out.astype(BF16), jnp.zeros((D,), F32), x, ln_g, ln_b, router)


PAD_ROWS = 8


def _causal_conv4(src, cw_ref, pad_ref):
    S = src.shape[0]
    pad_ref[0:PAD_ROWS, :] = jnp.zeros((PAD_ROWS, src.shape[1]), F32)
    pad_ref[PAD_ROWS:PAD_ROWS + S, :] = src
    out = None
    for k in range(4):
        off = PAD_ROWS - 3 + k
        term = cw_ref[k:k + 1, :] * pad_ref[off:off + S, :]
        out = term if out is None else out + term
    return out


def _lru_core_kernel(gate_ref, rec_ref, cw_ref, cb_ref, wa_ref, ba_ref, wx_ref, bx_ref, lam_ref, y_ref, pad_ref):
    S = rec_ref.shape[0]
    u = _causal_conv4(rec_ref[...].astype(F32), cw_ref, pad_ref) + cb_ref[...]
    ub = u.astype(BF16)
    r = jax.nn.sigmoid(_dot(ub, wa_ref[...]) + ba_ref[...])
    i = jax.nn.sigmoid(_dot(ub, wx_ref[...]) + bx_ref[...])
    log_a = -LRU_C * r * _softplus(-lam_ref[...])
    a = jnp.exp(log_a)
    th = jnp.tanh(log_a)
    b = jnp.sqrt(-2.0 * th / (1.0 - th)) * (i * u)
    h = _scan_rows(a, b, S)
    y_ref[...] = (jax.nn.gelu(gate_ref[...].astype(F32)) * h).astype(y_ref.dtype)


def _lru_core(gr, conv_w, conv_b, w_a, b_a, w_x, b_x, lam, batch, seq):
    nb = LRU_BLOCKS
    vec = lambda v: v.reshape(1, LRU_WIDTH)
    lane_blk = pl.BlockSpec((1, LANES), lambda b, n: (0, n))
    mat_blk = pl.BlockSpec((None, LANES, LANES), lambda b, n: (n, 0, 0))
    return pl.pallas_call(
        _lru_core_kernel,
        grid=(batch, nb),
        in_specs=[pl.BlockSpec((seq, LANES), lambda b, n: (b, n)),
                  pl.BlockSpec((seq, LANES), lambda b, n: (b, nb + n)),
                  pl.BlockSpec((LRU_CONV, LANES), lambda b, n: (0, n)),
                  lane_blk, mat_blk, lane_blk, mat_blk, lane_blk, lane_blk],
        out_specs=pl.BlockSpec((seq, LANES), lambda b, n: (b, n)),
        out_shape=jax.ShapeDtypeStruct((batch * seq, LRU_WIDTH), BF16),
        scratch_shapes=[pltpu.VMEM((seq + PAD_ROWS, LANES), F32)],
        compiler_params=_params("parallel", "parallel"),
        name="lru_core",
    )(gr, gr, conv_w, vec(conv_b), w_a.astype(BF16), vec(b_a), w_x.astype(BF16), vec(b_x), vec(lam))


def _lru_mixer(x, x_bf, batch, seq, w_in, conv_w, conv_b, w_a, b_a, w_x, b_x, lam, w_out, ln_g, ln_b,
               router=None):
    gr = _proj(x_bf, w_in.astype(BF16), BF16, tn=LRU_WIDTH)
    y = _lru_core(gr, conv_w, conv_b, w_a, b_a, w_x, b_x, lam, batch, seq)
    return _out_ln(y, w_out.astype(BF16), jnp.zeros((D_MODEL,), F32), x, ln_g, ln_b, router)


CONF_HALO = 32


def _conf_tail_kernel(h_ref, halo_ref, dw_ref, dwb_ref, cg_ref, cb_ref, w_ref, b_ref, res_ref, g_ref, beta_ref,
                      o_ref, obf_ref, buf_ref, *, tt):
    first = pl.program_id(1) == 0
    halo = halo_ref[...].astype(F32)
    buf_ref[0:CONF_HALO, :] = jnp.where(first, 0.0, halo)
    buf_ref[CONF_HALO:CONF_HALO + tt, :] = h_ref[...].astype(F32)
    first = CONF_HALO - (CONF_KERNEL - 1)
    conv = None
    for s in range(SUBLANES):
        offs = [o for o in range(first, first + CONF_KERNEL) if o % SUBLANES == s]
        span = offs[-1] - s + tt
        slab = buf_ref[s:s + span, :]
        part = None
        for o in offs:
            term = dw_ref[o - first:o - first + 1, :] * slab[o - s:o - s + tt]
            part = term if part is None else part + term
        conv = part if conv is None else conv + part
    hs = jax.nn.silu(_layer_norm(conv + dwb_ref[...], cg_ref[...], cb_ref[...]))
    mix = _dot(hs.astype(BF16), w_ref[...]) + b_ref[...]
    y = _layer_norm(ALPHA * res_ref[...] + mix, g_ref[...], beta_ref[...])
    o_ref[...] = y
    obf_ref[...] = y.astype(BF16)


def _conf_tail(h, dw_w, dw_b, cg, cb, w_out, b_out, res, ln_g, ln_b, batch, seq, tt=256):
    D = D_MODEL
    nt = seq // tt
    per = tt // CONF_HALO
    row = lambda b, i: (b * nt + i, 0)
    fixed = lambda b, i: (0, 0)
    vec = lambda v: v.reshape(1, D)
    return pl.pallas_call(
        functools.partial(_conf_tail_kernel, tt=tt),
        grid=(batch, nt),
        in_specs=[pl.BlockSpec((tt, D), row),
                  pl.BlockSpec((CONF_HALO, D), lambda b, i: (jnp.maximum((b * nt + i) * per - 1, 0), 0)),
                  pl.BlockSpec((CONF_KERNEL, D), fixed), pl.BlockSpec((1, D), fixed),
                  pl.BlockSpec((1, D), fixed), pl.BlockSpec((1, D), fixed),
                  pl.BlockSpec((D, D), fixed), pl.BlockSpec((1, D), fixed),
                  pl.BlockSpec((tt, D), row), pl.BlockSpec((1, D), fixed), pl.BlockSpec((1, D), fixed)],
        out_specs=[pl.BlockSpec((tt, D), row), pl.BlockSpec((tt, D), row)],
        out_shape=[jax.ShapeDtypeStruct((batch * seq, D), F32), jax.ShapeDtypeStruct((batch * seq, D), BF16)],
        scratch_shapes=[pltpu.VMEM((CONF_HALO + tt, D), F32)],
        compiler_params=_params("parallel", "arbitrary"),
        name="conf_tail",
    )(h, h, dw_w, vec(dw_b), vec(cg), vec(cb), w_out, vec(b_out), res, vec(ln_g), vec(ln_b))


def _conf_mixer(x, x_bf, batch, seq, w_in, b_in, dw_w, dw_b, cg, cb, w_out, b_out, ln_g, ln_b):
    h = _glu_proj(x_bf, w_in.astype(BF16), b_in, BF16)
    return _conf_tail(h, dw_w, dw_b, cg, cb, w_out.astype(BF16), b_out, x, ln_g, ln_b, batch, seq)


def _gdn_core_kernel(q_ref, k_ref, v_ref, z_ref, ba_ref, cwq_ref, cwk_ref, cwv_ref, alog_ref, dtb_ref, ng_ref,
                     pm_ref, o_ref,
                     pad_ref, qs_ref, ks_ref, kb_ref, vb_ref, kbe_ref, gcs_ref,
                     u_ref, w_ref, kend_ref, qg_ref, attn_ref, egl_ref, *, hb, tile):
    S = q_ref.shape[0]
    C = tile
    Dh = GDN_HEAD_DIM
    n_chunks = S // C
    grp = pl.program_id(1)
    lane = lax.broadcasted_iota(jnp.int32, (S, LANES), 1)

    ba = ba_ref[...]
    beta_all = jax.nn.sigmoid(ba)
    g_all = -jnp.exp(alog_ref[...]) * _softplus(ba + dtb_ref[...])
    gc_all = _scan_rows(None, g_all, C)

    def l2n(t):
        return t * lax.rsqrt(jnp.sum(t * t, axis=-1, keepdims=True) + 1e-6)

    for hh in range(hb):
        head = grp * hb + hh
        hs = slice(hh * Dh, (hh + 1) * Dh)
        beta = _lane_pick(beta_all, lane, head)
        gc = _lane_pick(gc_all, lane, GDN_HEADS + head)
        q = jax.nn.silu(_causal_conv4(q_ref[:, hs].astype(F32), cwq_ref.at[:, hs], pad_ref))
        k = jax.nn.silu(_causal_conv4(k_ref[:, hs].astype(F32), cwk_ref.at[:, hs], pad_ref))
        v = jax.nn.silu(_causal_conv4(v_ref[:, hs].astype(F32), cwv_ref.at[:, hs], pad_ref))
        q = l2n(q) * (Dh ** -0.5)
        k = l2n(k)
        eg = jnp.exp(gc)
        kb = k * beta
        qs_ref[hh] = q.astype(BF16)
        ks_ref[hh] = k.astype(BF16)
        kb_ref[hh] = kb.astype(BF16)
        vb_ref[hh] = (v * beta).astype(BF16)
        kbe_ref[hh] = (kb * eg).astype(BF16)
        qg_ref[hh] = (q * eg).astype(BF16)
        gcs_ref[hh] = jnp.broadcast_to(gc, (S, LANES))

    row = lax.broadcasted_iota(jnp.int32, (C, C), 0)
    col = lax.broadcasted_iota(jnp.int32, (C, C), 1)
    lane_c = lax.broadcasted_iota(jnp.int32, (C, LANES), 1)
    n_levels = pm_ref.shape[0]
    chunks_per_step = 2 if n_chunks % 2 == 0 else 1

    def split_features(gcl):
        hi = gcl.astype(BF16).astype(F32)
        r1 = gcl - hi
        mid = r1.astype(BF16).astype(F32)
        lo = r1 - mid
        fa = jnp.where(lane_c == 0, hi, jnp.where(lane_c == 1, mid, jnp.where(
            lane_c == 2, lo, jnp.where(lane_c < 6, 1.0, 0.0))))
        fb = jnp.where(lane_c < 3, 1.0, jnp.where(lane_c == 3, -hi, jnp.where(
            lane_c == 4, -mid, jnp.where(lane_c == 5, -lo, 0.0))))
        return fa.astype(BF16), fb.astype(BF16)

    def prep(n, _):
        chains = [(hh, n * chunks_per_step + cc) for cc in range(chunks_per_step) for hh in range(hb)]
        sls = [pl.ds(pl.multiple_of(cn * C, C), C) for _, cn in chains]
        heads = [hh for hh, _ in chains]
        ch = range(len(chains))
        kc = [ks_ref[heads[i], sls[i], :] for i in ch]
        gcl = [gcs_ref[heads[i], sls[i], :] for i in ch]
        feats = [split_features(gcl[i]) for i in ch]
        gdiff = [_dot_nt(*feats[i]) for i in ch]
        kk = [_dot_nt(kb_ref[heads[i], sls[i], :], kc[i]) for i in ch]
        qk = [_dot_nt(qs_ref[heads[i], sls[i], :], kc[i]) for i in ch]
        tril = row >= col
        decay = [jnp.where(tril, jnp.exp(jnp.where(tril, gdiff[i], 0.0)), 0.0) for i in ch]
        lower = [(kk[i] * decay[i]).astype(BF16) for i in ch]
        for i in ch:
            attn_ref[heads[i], sls[i], :] = (qk[i] * decay[i]).astype(BF16)
        eye = jnp.where(row == col, 1.0, 0.0)
        inv = [eye - (lower[i] * pm_ref[0]).astype(F32) for i in ch]
        for lvl in range(1, n_levels):
            inv_b = [inv[i].astype(BF16) for i in ch]
            part = [_dot(lower[i] * pm_ref[lvl], inv_b[i]).astype(BF16) for i in ch]
            inv = [inv[i] - _dot(inv_b[i], part[i]) for i in ch]
        inv_b = [inv[i].astype(BF16) for i in ch]
        u = [_dot(inv_b[i], vb_ref[heads[i], sls[i], :]) for i in ch]
        w = [_dot(inv_b[i], kbe_ref[heads[i], sls[i], :]) for i in ch]
        for i in ch:
            hh, cn = chains[i]
            u_ref[hh, sls[i], :] = u[i]
            w_ref[hh, sls[i], :] = w[i].astype(BF16)
            gl = gcl[i][C - 1:C, :]
            kend_ref[hh, sls[i], :] = (kc[i].astype(F32) * jnp.exp(gl - gcl[i])).astype(BF16)
            egl_ref[hh, pl.ds(pl.multiple_of(cn * 8, 8), 8), :] = jnp.broadcast_to(jnp.exp(gl), (8, LANES))
        return 0

    lax.fori_loop(0, n_chunks // chunks_per_step, prep, 0)

    ng = ng_ref[...]

    def step(n, states):
        sl = pl.ds(pl.multiple_of(n * C, C), C)
        hr = range(hb)
        st_b = [states[hh].astype(BF16) for hh in hr]
        w_st = [_dot(w_ref[hh, sl, :], st_b[hh]) for hh in hr]
        q_st = [_dot(qg_ref[hh, sl, :], st_b[hh]) for hh in hr]
        v_new_b = [(u_ref[hh, sl, :] - w_st[hh]).astype(BF16) for hh in hr]
        intra = [_dot(attn_ref[hh, sl, :], v_new_b[hh]) for hh in hr]
        grow = [_dot_tn(kend_ref[hh, sl, :], v_new_b[hh]) for hh in hr]
        new = []
        for hh in hr:
            egl = egl_ref[hh, pl.ds(pl.multiple_of(n * 8, 8), 8), :][0:1, 0:1]
            new.append(states[hh] * egl + grow[hh])
            o = q_st[hh] + intra[hh]
            o = o * lax.rsqrt(jnp.mean(o * o, axis=-1, keepdims=True) + RMS_EPS) * ng
            zc = z_ref[sl, hh * Dh:(hh + 1) * Dh].astype(F32)
            o_ref[sl, hh * Dh:(hh + 1) * Dh] = (o * jax.nn.silu(zc)).astype(o_ref.dtype)
        return tuple(new)

    lax.fori_loop(0, n_chunks, step, tuple(jnp.zeros((Dh, Dh), F32) for _ in range(hb)))


def _pair_masks(tile):
    r = jnp.arange(tile)[:, None]
    c = jnp.arange(tile)[None, :]
    levels = []
    m = 1
    while m < tile:
        levels.append(((r // m) == (c // m) + 1) & ((r // (2 * m)) == (c // (2 * m))))
        m *= 2
    return jnp.stack(levels).astype(BF16)


def _gdn_core(qkvz, ba, conv_w, a_log, dt_bias, norm_g, batch, seq, hb=2, tile=GDN_TILE):
    H, Dh = GDN_HEADS, GDN_HEAD_DIM
    W = hb * Dh
    ng_ = H // hb
    tile = min(tile, seq)
    masks = _pair_masks(tile)
    blk = lambda part: pl.BlockSpec((seq, W), lambda b, g: (b, part * ng_ + g))
    cw = lambda part: pl.BlockSpec((GDN_CONV, W), lambda b, g: (0, part * ng_ + g))
    fixed = pl.BlockSpec((1, LANES), lambda b, g: (0, 0))
    pad = lambda v: jnp.zeros((1, LANES), F32).at[0, H:2 * H].set(v)
    return pl.pallas_call(
        functools.partial(_gdn_core_kernel, hb=hb, tile=tile),
        grid=(batch, ng_),
        in_specs=[blk(0), blk(1), blk(2), blk(3), pl.BlockSpec((seq, LANES), lambda b, g: (b, 0)),
                  cw(0), cw(1), cw(2), fixed, fixed, fixed,
                  pl.BlockSpec(masks.shape, lambda b, g: (0, 0, 0))],
        out_specs=pl.BlockSpec((seq, W), lambda b, g: (b, g)),
        out_shape=jax.ShapeDtypeStruct((batch * seq, H * Dh), BF16),
        scratch_shapes=[pltpu.VMEM((seq + PAD_ROWS, Dh), F32)]
                       + [pltpu.VMEM((hb, seq, Dh), BF16)] * 5
                       + [pltpu.VMEM((hb, seq, LANES), F32),
                          pltpu.VMEM((hb, seq, Dh), F32),
                          pltpu.VMEM((hb, seq, Dh), BF16),
                          pltpu.VMEM((hb, seq, Dh), BF16),
                          pltpu.VMEM((hb, seq, Dh), BF16),
                          pltpu.VMEM((hb, seq, tile), BF16),
                          pltpu.VMEM((hb, seq // tile * 8, LANES), F32)],
        compiler_params=_params("parallel", "parallel"),
        name="gdn_core",
    )(qkvz, qkvz, qkvz, qkvz, ba, conv_w, conv_w, conv_w, pad(a_log), pad(dt_bias), norm_g.reshape(1, Dh), masks)


def _gdn_mixer(x, x_bf, batch, seq, w_in, conv_w, a_log, dt_bias, norm_g, w_out, ln_g, ln_b, router=None):
    HD = GDN_HEADS * GDN_HEAD_DIM
    qkvz = _proj(x_bf, w_in[:, :4 * HD].astype(BF16), BF16)
    ba = _proj(x_bf, _pad_lanes(w_in[:, 4 * HD:]).astype(BF16), F32, tn=LANES)
    o = _gdn_core(qkvz, ba, conv_w, a_log, dt_bias, norm_g, batch, seq)
    return _out_ln(o, w_out.astype(BF16), jnp.zeros((D_MODEL,), F32), x, ln_g, ln_b, router)


MOE_TILE = 1024


def _moe_expert_kernel(te_ref, tv_ref, x_ref, wg_ref, wu_ref, wd_ref, o_ref, xb_ref):
    i = pl.program_id(0)
    j = pl.program_id(1)
    valid = tv_ref[i] == 1
    half = MOE_TILE // 2
    halves = [pl.ds(0, half), pl.ds(half, half)]

    @pl.when(j == 0)
    def _():
        xb_ref[...] = x_ref[...].astype(BF16)
        o_ref[...] = jnp.zeros_like(o_ref)

    @pl.when(valid)
    def _():
        wg = wg_ref[...].astype(BF16)
        wu = wu_ref[...].astype(BF16)
        wd = wd_ref[...].astype(BF16)
        gate = [_dot(xb_ref[h, :], wg) for h in halves]
        up = [_dot(xb_ref[h, :], wu) for h in halves]
        hid = [(jax.nn.silu(gate[k]) * up[k]).astype(BF16) for k in range(2)]
        for k, h in enumerate(halves):
            o_ref[h, :] += _dot(hid[k], wd)


def _moe_experts(xs, tile_e, tile_valid, w_gu, w_down, layer, tf=512):
    n_rows, D = xs.shape
    n_tiles = n_rows // MOE_TILE
    nc = D_EXPERT // tf

    def chunk(j, tv, i):
        return jnp.where(tv[i] == 1, j, nc - 1)

    grid_spec = pltpu.PrefetchScalarGridSpec(
        num_scalar_prefetch=2,
        grid=(n_tiles, nc),
        in_specs=[pl.BlockSpec((MOE_TILE, D), lambda i, j, te, tv: (i, 0)),
                  pl.BlockSpec((None, None, D, tf), lambda i, j, te, tv: (layer, te[i], 0, chunk(j, tv, i))),
                  pl.BlockSpec((None, None, D, tf), lambda i, j, te, tv: (layer, te[i], 0, nc + chunk(j, tv, i))),
                  pl.BlockSpec((None, None, tf, D), lambda i, j, te, tv: (layer, te[i], chunk(j, tv, i), 0))],
        out_specs=pl.BlockSpec((MOE_TILE, D), lambda i, j, te, tv: (i, 0)),
        scratch_shapes=[pltpu.VMEM((MOE_TILE, D), BF16)],
    )
    return pl.pallas_call(
        _moe_expert_kernel,
        grid_spec=grid_spec,
        out_shape=jax.ShapeDtypeStruct((n_rows, D), F32),
        compiler_params=_params("arbitrary", "arbitrary"),
        name="moe_experts",
    )(tile_e, tile_valid, xs, w_gu, w_gu, w_down)


def _moe_ffn(x1, route, w_gu, w_down, layer, g, beta, wpg, p, wp):
    T, D = x1.shape
    n_pairs = T * TOP_K
    n_tiles = n_pairs // MOE_TILE + N_EXPERTS
    n_rows = n_tiles * MOE_TILE
    flat_e = route[:, :TOP_K].astype(jnp.int32).reshape(-1)
    onehot = (flat_e[:, None] == jnp.arange(N_EXPERTS, dtype=jnp.int32)[None, :]).astype(jnp.int32)
    csum = jnp.cumsum(onehot, axis=0)
    counts = csum[-1]
    rank = jnp.sum((csum - 1) * onehot, axis=1)
    padded = ((counts + MOE_TILE - 1) // MOE_TILE) * MOE_TILE
    pad_end = jnp.cumsum(padded)
    pad_start = pad_end - padded
    dest = pad_start[flat_e] + rank
    row_tok = jnp.zeros((n_rows,), jnp.int32).at[dest].set(jnp.arange(n_pairs, dtype=jnp.int32) // TOP_K)
    tile_start = jnp.arange(n_tiles, dtype=jnp.int32) * MOE_TILE
    tile_valid = tile_start < pad_end[-1]
    tile_e = jnp.minimum(jnp.searchsorted(pad_end, tile_start, side="right"), N_EXPERTS - 1).astype(jnp.int32)
    tile_e = jnp.where(tile_valid, tile_e, jnp.max(jnp.where(tile_valid, tile_e, 0)))
    xs = x1[row_tok]
    y = _moe_experts(xs, tile_e, tile_valid.astype(jnp.int32), w_gu, w_down, layer)
    dest2 = dest.reshape(T, TOP_K)
    return _moe_tail(x1, y[dest2[:, 0]], y[dest2[:, 1]], route, g, beta, wpg, p, wp)


def kernel(x, p, ln_mix_g, ln_mix_b, ln_ffn_g, ln_ffn_b, ple_w, ple_gate_w, fox_w_in, fox_b_f, fox_w_out, lru_w_in, lru_conv_w, lru_conv_b, lru_w_a, lru_b_a, lru_w_x, lru_b_x, lru_lambda, lru_w_out, cv_w_in, cv_b_in, cv_dw_w, cv_dw_b, cv_ln_g, cv_ln_b, cv_w_out, cv_b_out, gdn_w_in, gdn_conv_w, gdn_a_log, gdn_dt_bias, gdn_norm_g, gdn_w_out, ffn_w_gu, ffn_w_down, moe_w_router, moe_b_router, moe_w_gu, moe_w_down):
    B, S, D = x.shape
    groups = BATCH_GROUPS if B % BATCH_GROUPS == 0 else 1
    Bg = B // groups
    T = Bg * S
    xf = [x[gi * Bg:(gi + 1) * Bg].reshape(T, D) for gi in range(groups)]
    xb = list(xf)
    for i in range(DEPTH):
        m, j = i % 4, i // 4
        router = (moe_w_router[i // 2], moe_b_router[i // 2]) if i % 2 == 1 else None
        ln = (ln_mix_g[i], ln_mix_b[i])
        wpg = ple_gate_w[i].astype(BF16)
        wp = ple_w[i].astype(BF16)
        for gi in range(groups):
            args = (xf[gi], xb[gi], Bg, S)
            if m == 0:
                mixed = _fox_mixer(*args, fox_w_in[j], fox_b_f[j], fox_w_out[j], *ln, router)
            elif m == 1:
                mixed = _lru_mixer(*args, lru_w_in[j], lru_conv_w[j], lru_conv_b[j], lru_w_a[j], lru_b_a[j],
                                   lru_w_x[j], lru_b_x[j], lru_lambda[j], lru_w_out[j], *ln, router)
            elif m == 2:
                mixed = _conf_mixer(*args, cv_w_in[j], cv_b_in[j], cv_dw_w[j], cv_dw_b[j], cv_ln_g[j],
                                    cv_ln_b[j], cv_w_out[j], cv_b_out[j], *ln)
            else:
                mixed = _gdn_mixer(*args, gdn_w_in[j], gdn_conv_w[j], gdn_a_log[j], gdn_dt_bias[j],
                                   gdn_norm_g[j], gdn_w_out[j], *ln, router)
            tail = (ln_ffn_g[i], ln_ffn_b[i], wpg, p[i, gi * Bg:(gi + 1) * Bg].reshape(T, PLE_DIM), wp)
            if i % 2 == 0:
                x1, x1b = mixed
                xf[gi], xb[gi] = _ffn_tail(x1, x1b, ffn_w_gu[i // 2].astype(BF16),
                                           ffn_w_down[i // 2].astype(BF16), *tail)
            else:
                x1, x1b, route = mixed
                xf[gi], xb[gi] = _moe_ffn(x1, route, moe_w_gu, moe_w_down, i // 2, *tail)
    return jnp.concatenate([t.reshape(Bg, S, D) for t in xf], axis=0)
```

```python
import functools
import math

import jax
import jax.numpy as jnp
from jax import lax
from jax.experimental import pallas as pl
from jax.experimental.pallas import tpu as pltpu

F32 = jnp.float32
BF16 = jnp.bfloat16

D_MODEL = 1024
DEPTH = 4
PLE_DIM = 256
ALPHA = (2 * DEPTH) ** 0.25
LN_EPS = 1e-5
RMS_EPS = 1e-6
FOX_HEADS = 16
FOX_HEAD_DIM = 64
LRU_WIDTH = 1280
LRU_BLOCKS = 10
LRU_CONV = 4
LRU_C = 8.0
CONF_KERNEL = 31
GDN_HEADS = 8
GDN_HEAD_DIM = 128
GDN_CONV = 4
GDN_TILE = 256
D_FF = 2816
N_EXPERTS = 8
TOP_K = 2
D_EXPERT = 3584

LANES = 128
SUBLANES = 8
VMEM_LIMIT = 56 * 1024 * 1024
NEG_BIG = -1e30
BATCH_GROUPS = 2


def _params(*sem):
    return pltpu.CompilerParams(dimension_semantics=sem, vmem_limit_bytes=VMEM_LIMIT)


def _dot(a, b):
    return jnp.dot(a, b, preferred_element_type=F32)


def _dot_nt(a, b):
    return lax.dot_general(a, b, (((1,), (1,)), ((), ())), preferred_element_type=F32)


def _dot_tn(a, b):
    return lax.dot_general(a, b, (((0,), (0,)), ((), ())), preferred_element_type=F32)


def _layer_norm(y, g, b):
    mu = jnp.mean(y, axis=-1, keepdims=True)
    d = y - mu
    var = jnp.mean(d * d, axis=-1, keepdims=True)
    return d * lax.rsqrt(var + LN_EPS) * g + b


def _softplus(x):
    return jnp.maximum(x, 0.0) + jnp.log1p(jnp.exp(-jnp.abs(x)))


def _lane_pick(x, lane_idx, idx):
    return jnp.sum(jnp.where(lane_idx == idx, x, 0.0), axis=-1, keepdims=True)


def _scan_rows(a, b, period):
    rows = b.shape[0]
    t = lax.broadcasted_iota(jnp.int32, b.shape, 0)
    if period < rows:
        t = t & (period - 1)
    d = 1
    while d < period:
        if period == rows and d % SUBLANES == 0:
            if a is None:
                b = jnp.concatenate([b[:d], b[d:] + b[:-d]], axis=0)
            else:
                b = jnp.concatenate([b[:d], b[d:] + a[d:] * b[:-d]], axis=0)
                a = jnp.concatenate([a[:d], a[d:] * a[:-d]], axis=0)
            d *= 2
            continue
        ok = t >= d
        b_sh = jnp.where(ok, pltpu.roll(b, d, axis=0), 0.0)
        if a is None:
            b = b + b_sh
        else:
            b = b + a * b_sh
            a = a * jnp.where(ok, pltpu.roll(a, d, axis=0), 1.0)
        d *= 2
    return b


def _proj_kernel(x_ref, w_ref, o_ref):
    o_ref[...] = _dot(x_ref[...].astype(BF16), w_ref[...]).astype(o_ref.dtype)


def _proj(x, w, out_dtype, tm=1024, tn=1024):
    T, K = x.shape
    N = w.shape[1]
    tm = min(tm, T)
    tn = min(tn, N)
    return pl.pallas_call(
        _proj_kernel,
        grid=(T // tm, N // tn),
        in_specs=[pl.BlockSpec((tm, K), lambda i, j: (i, 0)),
                  pl.BlockSpec((K, tn), lambda i, j: (0, j))],
        out_specs=pl.BlockSpec((tm, tn), lambda i, j: (i, j)),
        out_shape=jax.ShapeDtypeStruct((T, N), out_dtype),
        compiler_params=_params("parallel", "arbitrary"),
        name="proj",
    )(x, w)


def _glu_proj_kernel(x_ref, wv_ref, wg_ref, bv_ref, bg_ref, o_ref):
    x = x_ref[...].astype(BF16)
    val = _dot(x, wv_ref[...]) + bv_ref[...]
    gate = _dot(x, wg_ref[...]) + bg_ref[...]
    o_ref[...] = (val * jax.nn.sigmoid(gate)).astype(o_ref.dtype)


def _glu_proj(x, w, b, out_dtype, tm=1024, tn=512):
    T, K = x.shape
    N = w.shape[1] // 2
    tm = min(tm, T)
    nj = N // tn
    b2 = b.reshape(1, 2 * N)
    return pl.pallas_call(
        _glu_proj_kernel,
        grid=(T // tm, nj),
        in_specs=[pl.BlockSpec((tm, K), lambda i, j: (i, 0)),
                  pl.BlockSpec((K, tn), lambda i, j: (0, j)),
                  pl.BlockSpec((K, tn), lambda i, j: (0, nj + j)),
                  pl.BlockSpec((1, tn), lambda i, j: (0, j)),
                  pl.BlockSpec((1, tn), lambda i, j: (0, nj + j))],
        out_specs=pl.BlockSpec((tm, tn), lambda i, j: (i, j)),
        out_shape=jax.ShapeDtypeStruct((T, N), out_dtype),
        compiler_params=_params("parallel", "arbitrary"),
        name="glu_proj",
    )(x, w, w, b2, b2)


def _out_ln_kernel(a_ref, w_ref, b_ref, res_ref, g_ref, beta_ref, o_ref, obf_ref):
    mix = _dot(a_ref[...], w_ref[...]) + b_ref[...]
    y = _layer_norm(ALPHA * res_ref[...] + mix, g_ref[...], beta_ref[...])
    o_ref[...] = y
    obf_ref[...] = y.astype(BF16)


def _out_ln_route_kernel(a_ref, w_ref, b_ref, res_ref, g_ref, beta_ref, wr_ref, br_ref,
                         o_ref, obf_ref, route_ref):
    mix = _dot(a_ref[...], w_ref[...]) + b_ref[...]
    y = _layer_norm(ALPHA * res_ref[...] + mix, g_ref[...], beta_ref[...])
    o_ref[...] = y
    obf_ref[...] = y.astype(BF16)
    l1 = jnp.sum(y * wr_ref[0:1, :], axis=-1, keepdims=True) + br_ref[0]
    i1 = jnp.zeros_like(l1)
    l2 = jnp.full_like(l1, -jnp.inf)
    i2 = jnp.zeros_like(l1)
    for e in range(1, N_EXPERTS):
        v = jnp.sum(y * wr_ref[e:e + 1, :], axis=-1, keepdims=True) + br_ref[e]
        gt1 = v > l1
        gt2 = v > l2
        l2 = jnp.where(gt1, l1, jnp.where(gt2, v, l2))
        i2 = jnp.where(gt1, i1, jnp.where(gt2, float(e), i2))
        l1 = jnp.where(gt1, v, l1)
        i1 = jnp.where(gt1, float(e), i1)
    e2 = jnp.exp(l2 - l1)
    w1 = 1.0 / (1.0 + e2)
    w2 = e2 / (1.0 + e2)
    lane = lax.broadcasted_iota(jnp.int32, route_ref.shape, 1)
    route_ref[...] = jnp.where(lane == 0, i1, jnp.where(lane == 1, i2, jnp.where(
        lane == 2, w1, jnp.where(lane == 3, w2, 0.0))))


def _out_ln(a, w, b, res, g, beta, router=None, tm=512):
    T, K = a.shape
    D = w.shape[1]
    row = lambda i: (i, 0)
    fixed = lambda i: (0, 0)
    in_specs = [pl.BlockSpec((tm, K), row), pl.BlockSpec((K, D), fixed), pl.BlockSpec((1, D), fixed),
                pl.BlockSpec((tm, D), row), pl.BlockSpec((1, D), fixed), pl.BlockSpec((1, D), fixed)]
    out_specs = [pl.BlockSpec((tm, D), row), pl.BlockSpec((tm, D), row)]
    out_shape = [jax.ShapeDtypeStruct((T, D), F32), jax.ShapeDtypeStruct((T, D), BF16)]
    args = [a, w, b.reshape(1, D), res, g.reshape(1, D), beta.reshape(1, D)]
    if router is None:
        body = _out_ln_kernel
    else:
        body = _out_ln_route_kernel
        w_r, b_r = router
        in_specs += [pl.BlockSpec((N_EXPERTS, D), fixed), pl.BlockSpec(memory_space=pltpu.SMEM)]
        out_specs.append(pl.BlockSpec((tm, LANES), row))
        out_shape.append(jax.ShapeDtypeStruct((T, LANES), F32))
        args += [w_r.T, b_r]
    return pl.pallas_call(
        body, grid=(T // tm,), in_specs=in_specs, out_specs=out_specs, out_shape=out_shape,
        compiler_params=_params("parallel"), name="out_ln",
    )(*args)


def _ln_ple(x1, ff, g, beta, wpg, p, wp):
    y = _layer_norm(ALPHA * x1 + ff, g, beta)
    gate = jax.nn.sigmoid(_dot(y.astype(BF16), wpg))
    return y + gate * _dot(p.astype(BF16), wp)


def _ffn_tail_kernel(x_ref, xbf_ref, wgu_ref, wd_ref, g_ref, beta_ref, wpg_ref, p_ref, wp_ref,
                     o_ref, obf_ref, *, tf):
    xb = xbf_ref[...]
    ff = None
    for c in range(D_FF // tf):
        gate = _dot(xb, wgu_ref[:, c * tf:(c + 1) * tf])
        up = _dot(xb, wgu_ref[:, D_FF + c * tf:D_FF + (c + 1) * tf])
        h = (jax.nn.silu(gate) * up).astype(BF16)
        part = _dot(h, wd_ref[c * tf:(c + 1) * tf, :])
        ff = part if ff is None else ff + part
    out = _ln_ple(x_ref[...], ff, g_ref[...], beta_ref[...], wpg_ref[...], p_ref[...], wp_ref[...])
    o_ref[...] = out
    obf_ref[...] = out.astype(BF16)


def _resident(shape):
    return pl.BlockSpec(shape, lambda i: (0,) * len(shape), pipeline_mode=pl.Buffered(1))


def _ffn_tail(x1, x1_bf, w_gu, w_down, g, beta, wpg, p, wp, tm=512, tf=256):
    T, D = x1.shape
    row = lambda i: (i, 0)
    return pl.pallas_call(
        functools.partial(_ffn_tail_kernel, tf=tf),
        grid=(T // tm,),
        in_specs=[pl.BlockSpec((tm, D), row), pl.BlockSpec((tm, D), row),
                  _resident((D, 2 * D_FF)), _resident((D_FF, D)),
                  _resident((1, D)), _resident((1, D)), _resident((D, D)),
                  pl.BlockSpec((tm, PLE_DIM), row), _resident((PLE_DIM, D))],
        out_specs=[pl.BlockSpec((tm, D), row), pl.BlockSpec((tm, D), row)],
        out_shape=[jax.ShapeDtypeStruct((T, D), F32), jax.ShapeDtypeStruct((T, D), BF16)],
        compiler_params=_params("parallel"),
        name="ffn_tail",
    )(x1, x1_bf, w_gu, w_down, g.reshape(1, D), beta.reshape(1, D), wpg, p, wp)


def _moe_tail_kernel(x_ref, y0_ref, y1_ref, route_ref, g_ref, beta_ref, wpg_ref, p_ref, wp_ref, o_ref, obf_ref):
    route = route_ref[...]
    lane = lax.broadcasted_iota(jnp.int32, route.shape, 1)
    ff = (y0_ref[...] * _lane_pick(route, lane, TOP_K) + y1_ref[...] * _lane_pick(route, lane, TOP_K + 1))
    out = _ln_ple(x_ref[...], ff, g_ref[...], beta_ref[...], wpg_ref[...], p_ref[...], wp_ref[...])
    o_ref[...] = out
    obf_ref[...] = out.astype(BF16)


def _moe_tail(x1, y0, y1, route, g, beta, wpg, p, wp, tm=512):
    T, D = x1.shape
    row = lambda i: (i, 0)
    return pl.pallas_call(
        _moe_tail_kernel,
        grid=(T // tm,),
        in_specs=[pl.BlockSpec((tm, D), row), pl.BlockSpec((tm, D), row), pl.BlockSpec((tm, D), row),
                  pl.BlockSpec((tm, LANES), row), _resident((1, D)), _resident((1, D)), _resident((D, D)),
                  pl.BlockSpec((tm, PLE_DIM), row), _resident((PLE_DIM, D))],
        out_specs=[pl.BlockSpec((tm, D), row), pl.BlockSpec((tm, D), row)],
        out_shape=[jax.ShapeDtypeStruct((T, D), F32), jax.ShapeDtypeStruct((T, D), BF16)],
        compiler_params=_params("parallel"),
        name="moe_tail",
    )(x1, y0, y1, route, g.reshape(1, D), beta.reshape(1, D), wpg, p, wp)


def _fox_decay_kernel(f_ref, bf_ref, c_ref):
    x = f_ref[...] + bf_ref[...]
    c_ref[...] = _scan_rows(None, -_softplus(-x), x.shape[0])


def _fox_decay(f_logit, b_f, batch, seq):
    return pl.pallas_call(
        _fox_decay_kernel,
        grid=(batch,),
        in_specs=[pl.BlockSpec((seq, LANES), lambda b: (b, 0)), pl.BlockSpec((1, LANES), lambda b: (0, 0))],
        out_specs=pl.BlockSpec((seq, LANES), lambda b: (b, 0)),
        out_shape=jax.ShapeDtypeStruct((batch * seq, LANES), F32),
        compiler_params=_params("parallel"),
        name="fox_decay",
    )(f_logit, b_f)


FOX_SUB = 128


def _fox_attn_kernel(q_ref, k_ref, v_ref, c_ref, o_ref, ckb_ref, vt_ref, *, tq):
    g = pl.program_id(1)
    qi = pl.program_id(2)
    S = k_ref.shape[0]
    ts = min(FOX_SUB, tq)
    nsub = tq // ts

    @pl.when(qi == 0)
    def _():
        c = c_ref[...]
        lane_s = lax.broadcasted_iota(jnp.int32, (S, LANES), 1)
        for hh in range(2):
            ckb_ref[hh] = jnp.broadcast_to(_lane_pick(c, lane_s, 2 * g + hh), (S, LANES))
        vt_ref[...] = v_ref[...].astype(F32).T.astype(BF16)

    lane = lax.broadcasted_iota(jnp.int32, (ts, LANES), 1)
    chains = [(i, hh) for i in range(nsub) for hh in range(2)]
    qs = []
    for i, hh in chains:
        q = q_ref[i * ts:(i + 1) * ts, :] * jnp.asarray(FOX_HEAD_DIM ** -0.5, BF16)
        qs.append(jnp.where((lane >= FOX_HEAD_DIM) == (hh == 1), q, jnp.zeros_like(q)))

    def block(j, carry, diagonal):
        start = pl.multiple_of(j * tq, tq)
        width = [(i + 1) * ts if diagonal else tq for i in range(nsub)]
        qk = [_dot_nt(k_ref[pl.ds(start, width[i]), :], qs[c]) for c, (i, hh) in enumerate(chains)]
        stats = []
        for c, (i, hh) in enumerate(chains):
            m, l, _ = carry[c]
            s = qk[c] - ckb_ref[hh, pl.ds(start, width[i]), :]
            if diagonal:
                key = lax.broadcasted_iota(jnp.int32, s.shape, 0)
                qry = lax.broadcasted_iota(jnp.int32, s.shape, 1)
                s = jnp.where(key <= qry + i * ts, s, NEG_BIG)
            m_new = jnp.maximum(m, jnp.max(s, axis=0, keepdims=True))
            scale = jnp.exp(m - m_new)
            pr = jnp.exp(s - m_new)
            stats.append((m_new, scale * l + jnp.sum(pr, axis=0, keepdims=True), scale, pr.astype(BF16)))
        pv = [_dot(vt_ref[:, pl.ds(start, width[i])], stats[c][3]) for c, (i, hh) in enumerate(chains)]
        return tuple((stats[c][0], stats[c][1], stats[c][2] * carry[c][2] + pv[c]) for c in range(len(chains)))

    init = tuple((jnp.full((1, ts), NEG_BIG, F32), jnp.zeros((1, ts), F32), jnp.zeros((LANES, ts), F32))
                 for _ in chains)
    carry = lax.fori_loop(0, qi, lambda j, c: block(j, c, False), init)
    final = block(qi, carry, True)
    dim = lax.broadcasted_iota(jnp.int32, (LANES, ts), 0)
    for i in range(nsub):
        (_, l0, a0), (_, l1, a1) = final[2 * i], final[2 * i + 1]
        out_t = jnp.where(dim < FOX_HEAD_DIM, a0 / l0, a1 / l1)
        o_ref[i * ts:(i + 1) * ts, :] = out_t.T.astype(o_ref.dtype)


def _fox_attn(qkv, c, batch, seq, tq=512):
    tq = min(tq, seq)
    nq = seq // tq
    pairs = FOX_HEADS // 2
    return pl.pallas_call(
        functools.partial(_fox_attn_kernel, tq=tq),
        grid=(batch, pairs, nq),
        in_specs=[pl.BlockSpec((tq, LANES), lambda b, g, i: (b * nq + i, g)),
                  pl.BlockSpec((seq, LANES), lambda b, g, i: (b, pairs + g)),
                  pl.BlockSpec((seq, LANES), lambda b, g, i: (b, 2 * pairs + g)),
                  pl.BlockSpec((seq, LANES), lambda b, g, i: (b, 0))],
        out_specs=pl.BlockSpec((tq, LANES), lambda b, g, i: (b * nq + i, g)),
        out_shape=jax.ShapeDtypeStruct((batch * seq, D_MODEL), BF16),
        scratch_shapes=[pltpu.VMEM((2, seq, LANES), F32), pltpu.VMEM((LANES, seq), BF16)],
        compiler_params=_params("parallel", "parallel", "arbitrary"),
        name="fox_attn",
    )(qkv, qkv, qkv, c)


def _pad_lanes(w):
    return jnp.zeros((w.shape[0], LANES), w.dtype).at[:, :w.shape[1]].set(w)


def _fox_mixer(x, x_bf, batch, seq, w_in, b_f, w_out, ln_g, ln_b, router=None):
    D = D_MODEL
    qkv = _proj(x_bf, w_in[:, :3 * D].astype(BF16), BF16)
    f_logit = _proj(x_bf, _pad_lanes(w_in[:, 3 * D:]).astype(BF16), F32, tn=LANES)
    c = _fox_decay(f_logit, _pad_lanes(b_f.reshape(1, -1)), batch, seq)
    o = _fox_attn(qkv, c, batch, seq)
    return _out_ln(o, w_out.astype(BF16), jnp.zeros((D,), F32), x, ln_g, ln_b, router)


PAD_ROWS = 8


def _causal_conv4(src, cw_ref, pad_ref):
    S = src.shape[0]
    pad_ref[0:PAD_ROWS, :] = jnp.zeros((PAD_ROWS, src.shape[1]), F32)
    pad_ref[PAD_ROWS:PAD_ROWS + S, :] = src
    out = None
    for k in range(4):
        off = PAD_ROWS - 3 + k
        term = cw_ref[k:k + 1, :] * pad_ref[off:off + S, :]
        out = term if out is None else out + term
    return out


def _lru_core_kernel(gate_ref, rec_ref, cw_ref, cb_ref, wa_ref, ba_ref, wx_ref, bx_ref, lam_ref, y_ref, pad_ref):
    S = rec_ref.shape[0]
    u = _causal_conv4(rec_ref[...].astype(F32), cw_ref, pad_ref) + cb_ref[...]
    ub = u.astype(BF16)
    r = jax.nn.sigmoid(_dot(ub, wa_ref[...]) + ba_ref[...])
    i = jax.nn.sigmoid(_dot(ub, wx_ref[...]) + bx_ref[...])
    log_a = -LRU_C * r * _softplus(-lam_ref[...])
    a = jnp.exp(log_a)
    th = jnp.tanh(log_a)
    b = jnp.sqrt(-2.0 * th / (1.0 - th)) * (i * u)
    h = _scan_rows(a, b, S)
    y_ref[...] = (jax.nn.gelu(gate_ref[...].astype(F32)) * h).astype(y_ref.dtype)


def _lru_core(gr, conv_w, conv_b, w_a, b_a, w_x, b_x, lam, batch, seq):
    nb = LRU_BLOCKS
    vec = lambda v: v.reshape(1, LRU_WIDTH)
    lane_blk = pl.BlockSpec((1, LANES), lambda b, n: (0, n))
    mat_blk = pl.BlockSpec((None, LANES, LANES), lambda b, n: (n, 0, 0))
    return pl.pallas_call(
        _lru_core_kernel,
        grid=(batch, nb),
        in_specs=[pl.BlockSpec((seq, LANES), lambda b, n: (b, n)),
                  pl.BlockSpec((seq, LANES), lambda b, n: (b, nb + n)),
                  pl.BlockSpec((LRU_CONV, LANES), lambda b, n: (0, n)),
                  lane_blk, mat_blk, lane_blk, mat_blk, lane_blk, lane_blk],
        out_specs=pl.BlockSpec((seq, LANES), lambda b, n: (b, n)),
        out_shape=jax.ShapeDtypeStruct((batch * seq, LRU_WIDTH), BF16),
        scratch_shapes=[pltpu.VMEM((seq + PAD_ROWS, LANES), F32)],
        compiler_params=_params("parallel", "parallel"),
        name="lru_core",
    )(gr, gr, conv_w, vec(conv_b), w_a.astype(BF16), vec(b_a), w_x.astype(BF16), vec(b_x), vec(lam))


def _lru_mixer(x, x_bf, batch, seq, w_in, conv_w, conv_b, w_a, b_a, w_x, b_x, lam, w_out, ln_g, ln_b,
               router=None):
    gr = _proj(x_bf, w_in.astype(BF16), BF16, tn=LRU_WIDTH)
    y = _lru_core(gr, conv_w, conv_b, w_a, b_a, w_x, b_x, lam, batch, seq)
    return _out_ln(y, w_out.astype(BF16), jnp.zeros((D_MODEL,), F32), x, ln_g, ln_b, router)


CONF_HALO = 32


def _conf_tail_kernel(h_ref, halo_ref, dw_ref, dwb_ref, cg_ref, cb_ref, w_ref, b_ref, res_ref, g_ref, beta_ref,
                      o_ref, obf_ref, buf_ref, *, tt):
    first = pl.program_id(1) == 0
    halo = halo_ref[...].astype(F32)
    buf_ref[0:CONF_HALO, :] = jnp.where(first, 0.0, halo)
    buf_ref[CONF_HALO:CONF_HALO + tt, :] = h_ref[...].astype(F32)
    first = CONF_HALO - (CONF_KERNEL - 1)
    conv = None
    for s in range(SUBLANES):
        offs = [o for o in range(first, first + CONF_KERNEL) if o % SUBLANES == s]
        span = offs[-1] - s + tt
        slab = buf_ref[s:s + span, :]
        part = None
        for o in offs:
            term = dw_ref[o - first:o - first + 1, :] * slab[o - s:o - s + tt]
            part = term if part is None else part + term
        conv = part if conv is None else conv + part
    hs = jax.nn.silu(_layer_norm(conv + dwb_ref[...], cg_ref[...], cb_ref[...]))
    mix = _dot(hs.astype(BF16), w_ref[...]) + b_ref[...]
    y = _layer_norm(ALPHA * res_ref[...] + mix, g_ref[...], beta_ref[...])
    o_ref[...] = y
    obf_ref[...] = y.astype(BF16)


def _conf_tail(h, dw_w, dw_b, cg, cb, w_out, b_out, res, ln_g, ln_b, batch, seq, tt=256):
    D = D_MODEL
    nt = seq // tt
    per = tt // CONF_HALO
    row = lambda b, i: (b * nt + i, 0)
    fixed = lambda b, i: (0, 0)
    vec = lambda v: v.reshape(1, D)
    return pl.pallas_call(
        functools.partial(_conf_tail_kernel, tt=tt),
        grid=(batch, nt),
        in_specs=[pl.BlockSpec((tt, D), row),
                  pl.BlockSpec((CONF_HALO, D), lambda b, i: (jnp.maximum((b * nt + i) * per - 1, 0), 0)),
                  pl.BlockSpec((CONF_KERNEL, D), fixed), pl.BlockSpec((1, D), fixed),
                  pl.BlockSpec((1, D), fixed), pl.BlockSpec((1, D), fixed),
                  pl.BlockSpec((D, D), fixed), pl.BlockSpec((1, D), fixed),
                  pl.BlockSpec((tt, D), row), pl.BlockSpec((1, D), fixed), pl.BlockSpec((1, D), fixed)],
        out_specs=[pl.BlockSpec((tt, D), row), pl.BlockSpec((tt, D), row)],
        out_shape=[jax.ShapeDtypeStruct((batch * seq, D), F32), jax.ShapeDtypeStruct((batch * seq, D), BF16)],
        scratch_shapes=[pltpu.VMEM((CONF_HALO + tt, D), F32)],
        compiler_params=_params("parallel", "arbitrary"),
        name="conf_tail",
    )(h, h, dw_w, vec(dw_b), vec(cg), vec(cb), w_out, vec(b_out), res, vec(ln_g), vec(ln_b))


def _conf_mixer(x, x_bf, batch, seq, w_in, b_in, dw_w, dw_b, cg, cb, w_out, b_out, ln_g, ln_b):
    h = _glu_proj(x_bf, w_in.astype(BF16), b_in, BF16)
    return _conf_tail(h, dw_w, dw_b, cg, cb, w_out.astype(BF16), b_out, x, ln_g, ln_b, batch, seq)


def _gdn_core_kernel(q_ref, k_ref, v_ref, z_ref, ba_ref, cwq_ref, cwk_ref, cwv_ref, alog_ref, dtb_ref, ng_ref,
                     pm_ref, o_ref,
                     pad_ref, qs_ref, ks_ref, kb_ref, vb_ref, kbe_ref, gcs_ref,
                     u_ref, w_ref, kend_ref, qg_ref, attn_ref, egl_ref, *, hb, tile):
    S = q_ref.shape[0]
    C = tile
    Dh = GDN_HEAD_DIM
    n_chunks = S // C
    grp = pl.program_id(1)
    lane = lax.broadcasted_iota(jnp.int32, (S, LANES), 1)

    ba = ba_ref[...]
    beta_all = jax.nn.sigmoid(ba)
    g_all = -jnp.exp(alog_ref[...]) * _softplus(ba + dtb_ref[...])
    gc_all = _scan_rows(None, g_all, C)

    def l2n(t):
        return t * lax.rsqrt(jnp.sum(t * t, axis=-1, keepdims=True) + 1e-6)

    for hh in range(hb):
        head = grp * hb + hh
        hs = slice(hh * Dh, (hh + 1) * Dh)
        beta = _lane_pick(beta_all, lane, head)
        gc = _lane_pick(gc_all, lane, GDN_HEADS + head)
        q = jax.nn.silu(_causal_conv4(q_ref[:, hs].astype(F32), cwq_ref.at[:, hs], pad_ref))
        k = jax.nn.silu(_causal_conv4(k_ref[:, hs].astype(F32), cwk_ref.at[:, hs], pad_ref))
        v = jax.nn.silu(_causal_conv4(v_ref[:, hs].astype(F32), cwv_ref.at[:, hs], pad_ref))
        q = l2n(q) * (Dh ** -0.5)
        k = l2n(k)
        eg = jnp.exp(gc)
        kb = k * beta
        qs_ref[hh] = q.astype(BF16)
        ks_ref[hh] = k.astype(BF16)
        kb_ref[hh] = kb.astype(BF16)
        vb_ref[hh] = (v * beta).astype(BF16)
        kbe_ref[hh] = (kb * eg).astype(BF16)
        qg_ref[hh] = (q * eg).astype(BF16)
        gcs_ref[hh] = jnp.broadcast_to(gc, (S, LANES))

    row = lax.broadcasted_iota(jnp.int32, (C, C), 0)
    col = lax.broadcasted_iota(jnp.int32, (C, C), 1)
    lane_c = lax.broadcasted_iota(jnp.int32, (C, LANES), 1)
    n_levels = pm_ref.shape[0]
    chunks_per_step = 2 if n_chunks % 2 == 0 else 1

    def split_features(gcl):
        hi = gcl.astype(BF16).astype(F32)
        r1 = gcl - hi
        mid = r1.astype(BF16).astype(F32)
        lo = r1 - mid
        fa = jnp.where(lane_c == 0, hi, jnp.where(lane_c == 1, mid, jnp.where(
            lane_c == 2, lo, jnp.where(lane_c < 6, 1.0, 0.0))))
        fb = jnp.where(lane_c < 3, 1.0, jnp.where(lane_c == 3, -hi, jnp.where(
            lane_c == 4, -mid, jnp.where(lane_c == 5, -lo, 0.0))))
        return fa.astype(BF16), fb.astype(BF16)

    def prep(n, _):
        chains = [(hh, n * chunks_per_step + cc) for cc in range(chunks_per_step) for hh in range(hb)]
        sls = [pl.ds(pl.multiple_of(cn * C, C), C) for _, cn in chains]
        heads = [hh for hh, _ in chains]
        ch = range(len(chains))
        kc = [ks_ref[heads[i], sls[i], :] for i in ch]
        gcl = [gcs_ref[heads[i], sls[i], :] for i in ch]
        feats = [split_features(gcl[i]) for i in ch]
        gdiff = [_dot_nt(*feats[i]) for i in ch]
        kk = [_dot_nt(kb_ref[heads[i], sls[i], :], kc[i]) for i in ch]
        qk = [_dot_nt(qs_ref[heads[i], sls[i], :], kc[i]) for i in ch]
        tril = row >= col
        decay = [jnp.where(tril, jnp.exp(jnp.where(tril, gdiff[i], 0.0)), 0.0) for i in ch]
        lower = [(kk[i] * decay[i]).astype(BF16) for i in ch]
        for i in ch:
            attn_ref[heads[i], sls[i], :] = (qk[i] * decay[i]).astype(BF16)
        eye = jnp.where(row == col, 1.0, 0.0)
        inv = [eye - (lower[i] * pm_ref[0]).astype(F32) for i in ch]
        for lvl in range(1, n_levels):
            inv_b = [inv[i].astype(BF16) for i in ch]
            part = [_dot(lower[i] * pm_ref[lvl], inv_b[i]).astype(BF16) for i in ch]
            inv = [inv[i] - _dot(inv_b[i], part[i]) for i in ch]
        inv_b = [inv[i].astype(BF16) for i in ch]
        u = [_dot(inv_b[i], vb_ref[heads[i], sls[i], :]) for i in ch]
        w = [_dot(inv_b[i], kbe_ref[heads[i], sls[i], :]) for i in ch]
        for i in ch:
            hh, cn = chains[i]
            u_ref[hh, sls[i], :] = u[i]
            w_ref[hh, sls[i], :] = w[i].astype(BF16)
            gl = gcl[i][C - 1:C, :]
            kend_ref[hh, sls[i], :] = (kc[i].astype(F32) * jnp.exp(gl - gcl[i])).astype(BF16)
            egl_ref[hh, pl.ds(pl.multiple_of(cn * 8, 8), 8), :] = jnp.broadcast_to(jnp.exp(gl), (8, LANES))
        return 0

    lax.fori_loop(0, n_chunks // chunks_per_step, prep, 0)

    ng = ng_ref[...]

    def step(n, states):
        sl = pl.ds(pl.multiple_of(n * C, C), C)
        hr = range(hb)
        st_b = [states[hh].astype(BF16) for hh in hr]
        w_st = [_dot(w_ref[hh, sl, :], st_b[hh]) for hh in hr]
        q_st = [_dot(qg_ref[hh, sl, :], st_b[hh]) for hh in hr]
        v_new_b = [(u_ref[hh, sl, :] - w_st[hh]).astype(BF16) for hh in hr]
        intra = [_dot(attn_ref[hh, sl, :], v_new_b[hh]) for hh in hr]
        grow = [_dot_tn(kend_ref[hh, sl, :], v_new_b[hh]) for hh in hr]
        new = []
        for hh in hr:
            egl = egl_ref[hh, pl.ds(pl.multiple_of(n * 8, 8), 8), :][0:1, 0:1]
            new.append(states[hh] * egl + grow[hh])
            o = q_st[hh] + intra[hh]
            o = o * lax.rsqrt(jnp.mean(o * o, axis=-1, keepdims=True) + RMS_EPS) * ng
            zc = z_ref[sl, hh * Dh:(hh + 1) * Dh].astype(F32)
            o_ref[sl, hh * Dh:(hh + 1) * Dh] = (o * jax.nn.silu(zc)).astype(o_ref.dtype)
        return tuple(new)

    lax.fori_loop(0, n_chunks, step, tuple(jnp.zeros((Dh, Dh), F32) for _ in range(hb)))


def _pair_masks(tile):
    r = jnp.arange(tile)[:, None]
    c = jnp.arange(tile)[None, :]
    levels = []
    m = 1
    while m < tile:
        levels.append(((r // m) == (c // m) + 1) & ((r // (2 * m)) == (c // (2 * m))))
        m *= 2
    return jnp.stack(levels).astype(BF16)


def _gdn_core(qkvz, ba, conv_w, a_log, dt_bias, norm_g, batch, seq, hb=2, tile=GDN_TILE):
    H, Dh = GDN_HEADS, GDN_HEAD_DIM
    W = hb * Dh
    ng_ = H // hb
    tile = min(tile, seq)
    masks = _pair_masks(tile)
    blk = lambda part: pl.BlockSpec((seq, W), lambda b, g: (b, part * ng_ + g))
    cw = lambda part: pl.BlockSpec((GDN_CONV, W), lambda b, g: (0, part * ng_ + g))
    fixed = pl.BlockSpec((1, LANES), lambda b, g: (0, 0))
    pad = lambda v: jnp.zeros((1, LANES), F32).at[0, H:2 * H].set(v)
    return pl.pallas_call(
        functools.partial(_gdn_core_kernel, hb=hb, tile=tile),
        grid=(batch, ng_),
        in_specs=[blk(0), blk(1), blk(2), blk(3), pl.BlockSpec((seq, LANES), lambda b, g: (b, 0)),
                  cw(0), cw(1), cw(2), fixed, fixed, fixed,
                  pl.BlockSpec(masks.shape, lambda b, g: (0, 0, 0))],
        out_specs=pl.BlockSpec((seq, W), lambda b, g: (b, g)),
        out_shape=jax.ShapeDtypeStruct((batch * seq, H * Dh), BF16),
        scratch_shapes=[pltpu.VMEM((seq + PAD_ROWS, Dh), F32)]
                       + [pltpu.VMEM((hb, seq, Dh), BF16)] * 5
                       + [pltpu.VMEM((hb, seq, LANES), F32),
                          pltpu.VMEM((hb, seq, Dh), F32),
                          pltpu.VMEM((hb, seq, Dh), BF16),
                          pltpu.VMEM((hb, seq, Dh), BF16),
                          pltpu.VMEM((hb, seq, Dh), BF16),
                          pltpu.VMEM((hb, seq, tile), BF16),
                          pltpu.VMEM((hb, seq // tile * 8, LANES), F32)],
        compiler_params=_params("parallel", "parallel"),
        name="gdn_core",
    )(qkvz, qkvz, qkvz, qkvz, ba, conv_w, conv_w, conv_w, pad(a_log), pad(dt_bias), norm_g.reshape(1, Dh), masks)


def _gdn_mixer(x, x_bf, batch, seq, w_in, conv_w, a_log, dt_bias, norm_g, w_out, ln_g, ln_b, router=None):
    HD = GDN_HEADS * GDN_HEAD_DIM
    qkvz = _proj(x_bf, w_in[:, :4 * HD].astype(BF16), BF16)
    ba = _proj(x_bf, _pad_lanes(w_in[:, 4 * HD:]).astype(BF16), F32, tn=LANES)
    o = _gdn_core(qkvz, ba, conv_w, a_log, dt_bias, norm_g, batch, seq)
    return _out_ln(o, w_out.astype(BF16), jnp.zeros((D_MODEL,), F32), x, ln_g, ln_b, router)


MOE_TILE = 1024


def _moe_expert_kernel(te_ref, tr_ref, x_ref, wg_ref, wu_ref, wd_ref, o_ref, xb_ref):
    i = pl.program_id(0)
    j = pl.program_id(1)
    rows = tr_ref[i]
    half = MOE_TILE // 2
    halves = [pl.ds(0, half), pl.ds(half, half)]

    @pl.when(j == 0)
    def _():
        xb_ref[...] = x_ref[...].astype(BF16)
        o_ref[...] = jnp.zeros_like(o_ref)

    def swiglu_chunk(parts):
        wg = wg_ref[...].astype(BF16)
        wu = wu_ref[...].astype(BF16)
        wd = wd_ref[...].astype(BF16)
        gate = [_dot(xb_ref[h, :], wg) for h in parts]
        up = [_dot(xb_ref[h, :], wu) for h in parts]
        hid = [(jax.nn.silu(gate[k]) * up[k]).astype(BF16) for k in range(len(parts))]
        for k, h in enumerate(parts):
            o_ref[h, :] += _dot(hid[k], wd)

    @pl.when(rows > half)
    def _():
        swiglu_chunk(halves)

    @pl.when((rows > 0) & (rows <= half))
    def _():
        swiglu_chunk(halves[:1])


def _moe_experts(xs, tile_e, tile_rows, w_gu, w_down, layer, tf=512):
    n_rows, D = xs.shape
    n_tiles = n_rows // MOE_TILE
    nc = D_EXPERT // tf

    def chunk(j, tr, i):
        return jnp.where(tr[i] > 0, j, nc - 1)

    grid_spec = pltpu.PrefetchScalarGridSpec(
        num_scalar_prefetch=2,
        grid=(n_tiles, nc),
        in_specs=[pl.BlockSpec((MOE_TILE, D), lambda i, j, te, tv: (i, 0)),
                  pl.BlockSpec((None, None, D, tf), lambda i, j, te, tv: (layer, te[i], 0, chunk(j, tv, i))),
                  pl.BlockSpec((None, None, D, tf), lambda i, j, te, tv: (layer, te[i], 0, nc + chunk(j, tv, i))),
                  pl.BlockSpec((None, None, tf, D), lambda i, j, te, tv: (layer, te[i], chunk(j, tv, i), 0))],
        out_specs=pl.BlockSpec((MOE_TILE, D), lambda i, j, te, tv: (i, 0)),
        scratch_shapes=[pltpu.VMEM((MOE_TILE, D), BF16)],
    )
    return pl.pallas_call(
        _moe_expert_kernel,
        grid_spec=grid_spec,
        out_shape=jax.ShapeDtypeStruct((n_rows, D), F32),
        compiler_params=_params("arbitrary", "arbitrary"),
        name="moe_experts",
    )(tile_e, tile_rows, xs, w_gu, w_gu, w_down)


def _moe_ffn(x1, route, w_gu, w_down, layer, g, beta, wpg, p, wp):
    T, D = x1.shape
    n_pairs = T * TOP_K
    n_tiles = n_pairs // MOE_TILE + N_EXPERTS
    n_rows = n_tiles * MOE_TILE
    flat_e = route[:, :TOP_K].astype(jnp.int32).reshape(-1)
    onehot = (flat_e[:, None] == jnp.arange(N_EXPERTS, dtype=jnp.int32)[None, :]).astype(jnp.int32)
    csum = jnp.cumsum(onehot, axis=0)
    counts = csum[-1]
    rank = jnp.sum((csum - 1) * onehot, axis=1)
    padded = ((counts + MOE_TILE - 1) // MOE_TILE) * MOE_TILE
    pad_end = jnp.cumsum(padded)
    pad_start = pad_end - padded
    dest = pad_start[flat_e] + rank
    row_tok = jnp.zeros((n_rows,), jnp.int32).at[dest].set(jnp.arange(n_pairs, dtype=jnp.int32) // TOP_K)
    tile_start = jnp.arange(n_tiles, dtype=jnp.int32) * MOE_TILE
    tile_valid = tile_start < pad_end[-1]
    tile_e = jnp.minimum(jnp.searchsorted(pad_end, tile_start, side="right"), N_EXPERTS - 1).astype(jnp.int32)
    tile_rows = jnp.clip((pad_start + counts)[tile_e] - tile_start, 0, MOE_TILE)
    tile_rows = jnp.where(tile_valid, tile_rows, 0).astype(jnp.int32)
    tile_e = jnp.where(tile_valid, tile_e, jnp.max(jnp.where(tile_valid, tile_e, 0)))
    y = _moe_experts(x1[row_tok], tile_e, tile_rows, w_gu, w_down, layer)
    dest2 = dest.reshape(T, TOP_K)
    return _moe_tail(x1, y[dest2[:, 0]], y[dest2[:, 1]], route, g, beta, wpg, p, wp)


def kernel(x, p, ln_mix_g, ln_mix_b, ln_ffn_g, ln_ffn_b, ple_w, ple_gate_w, fox_w_in, fox_b_f, fox_w_out, lru_w_in, lru_conv_w, lru_conv_b, lru_w_a, lru_b_a, lru_w_x, lru_b_x, lru_lambda, lru_w_out, cv_w_in, cv_b_in, cv_dw_w, cv_dw_b, cv_ln_g, cv_ln_b, cv_w_out, cv_b_out, gdn_w_in, gdn_conv_w, gdn_a_log, gdn_dt_bias, gdn_norm_g, gdn_w_out, ffn_w_gu, ffn_w_down, moe_w_router, moe_b_router, moe_w_gu, moe_w_down):
    B, S, D = x.shape
    groups = BATCH_GROUPS if B % BATCH_GROUPS == 0 else 1
    Bg = B // groups
    T = Bg * S
    xf = [x[gi * Bg:(gi + 1) * Bg].reshape(T, D) for gi in range(groups)]
    xb = list(xf)
    for i in range(DEPTH):
        m, j = i % 4, i // 4
        router = (moe_w_router[i // 2], moe_b_router[i // 2]) if i % 2 == 1 else None
        ln = (ln_mix_g[i], ln_mix_b[i])
        wpg = ple_gate_w[i].astype(BF16)
        wp = ple_w[i].astype(BF16)
        for gi in range(groups):
            args = (xf[gi], xb[gi], Bg, S)
            if m == 0:
                mixed = _fox_mixer(*args, fox_w_in[j], fox_b_f[j], fox_w_out[j], *ln, router)
            elif m == 1:
                mixed = _lru_mixer(*args, lru_w_in[j], lru_conv_w[j], lru_conv_b[j], lru_w_a[j], lru_b_a[j],
                                   lru_w_x[j], lru_b_x[j], lru_lambda[j], lru_w_out[j], *ln, router)
            elif m == 2:
                mixed = _conf_mixer(*args, cv_w_in[j], cv_b_in[j], cv_dw_w[j], cv_dw_b[j], cv_ln_g[j],
                                    cv_ln_b[j], cv_w_out[j], cv_b_out[j], *ln)
            else:
                mixed = _gdn_mixer(*args, gdn_w_in[j], gdn_conv_w[j], gdn_a_log[j], gdn_dt_bias[j],
                                   gdn_norm_g[j], gdn_w_out[j], *ln, router)
            tail = (ln_ffn_g[i], ln_ffn_b[i], wpg, p[i, gi * Bg:(gi + 1) * Bg].reshape(T, PLE_DIM), wp)
            if i % 2 == 0:
                x1, x1b = mixed
                xf[gi], xb[gi] = _ffn_tail(x1, x1b, ffn_w_gu[i // 2].astype(BF16),
                                           ffn_w_down[i // 2].astype(BF16), *tail)
            else:
                x1, x1b, route = mixed
                xf[gi], xb[gi] = _moe_ffn(x1, route, moe_w_gu, moe_w_down, i // 2, *tail)
    return jnp.concatenate([t.reshape(Bg, S, D) for t in xf], axis=0)
```

```python
import functools
import math

import jax
import jax.numpy as jnp
from jax import lax
from jax.experimental import pallas as pl
from jax.experimental.pallas import tpu as pltpu

F32 = jnp.float32
BF16 = jnp.bfloat16

D_MODEL = 1024
DEPTH = 4
PLE_DIM = 256
ALPHA = (2 * DEPTH) ** 0.25
LN_EPS = 1e-5
RMS_EPS = 1e-6
FOX_HEADS = 16
FOX_HEAD_DIM = 64
LRU_WIDTH = 1280
LRU_BLOCKS = 10
LRU_CONV = 4
LRU_C = 8.0
CONF_KERNEL = 31
GDN_HEADS = 8
GDN_HEAD_DIM = 128
GDN_CONV = 4
GDN_TILE = 256
D_FF = 2816
N_EXPERTS = 8
TOP_K = 2
D_EXPERT = 3584

LANES = 128
SUBLANES = 8
VMEM_LIMIT = 56 * 1024 * 1024
NEG_BIG = -1e30
BATCH_GROUPS = 2


def _params(*sem):
    return pltpu.CompilerParams(dimension_semantics=sem, vmem_limit_bytes=VMEM_LIMIT)


def _dot(a, b):
    return jnp.dot(a, b, preferred_element_type=F32)


def _dot_nt(a, b):
    return lax.dot_general(a, b, (((1,), (1,)), ((), ())), preferred_element_type=F32)


def _dot_tn(a, b):
    return lax.dot_general(a, b, (((0,), (0,)), ((), ())), preferred_element_type=F32)


def _layer_norm(y, g, b):
    mu = jnp.mean(y, axis=-1, keepdims=True)
    d = y - mu
    var = jnp.mean(d * d, axis=-1, keepdims=True)
    return d * lax.rsqrt(var + LN_EPS) * g + b


def _softplus(x):
    return jnp.maximum(x, 0.0) + jnp.log1p(jnp.exp(-jnp.abs(x)))


def _lane_pick(x, lane_idx, idx):
    return jnp.sum(jnp.where(lane_idx == idx, x, 0.0), axis=-1, keepdims=True)


def _scan_rows(a, b, period):
    rows = b.shape[0]
    t = lax.broadcasted_iota(jnp.int32, b.shape, 0)
    if period < rows:
        t = t & (period - 1)
    d = 1
    while d < period:
        if period == rows and d % SUBLANES == 0:
            if a is None:
                b = jnp.concatenate([b[:d], b[d:] + b[:-d]], axis=0)
            else:
                b = jnp.concatenate([b[:d], b[d:] + a[d:] * b[:-d]], axis=0)
                a = jnp.concatenate([a[:d], a[d:] * a[:-d]], axis=0)
            d *= 2
            continue
        ok = t >= d
        b_sh = jnp.where(ok, pltpu.roll(b, d, axis=0), 0.0)
        if a is None:
            b = b + b_sh
        else:
            b = b + a * b_sh
            a = a * jnp.where(ok, pltpu.roll(a, d, axis=0), 1.0)
        d *= 2
    return b


def _proj_kernel(x_ref, w_ref, o_ref):
    o_ref[...] = _dot(x_ref[...].astype(BF16), w_ref[...]).astype(o_ref.dtype)


def _proj(x, w, out_dtype, tm=1024, tn=1024):
    T, K = x.shape
    N = w.shape[1]
    tm = min(tm, T)
    tn = min(tn, N)
    return pl.pallas_call(
        _proj_kernel,
        grid=(T // tm, N // tn),
        in_specs=[pl.BlockSpec((tm, K), lambda i, j: (i, 0)),
                  pl.BlockSpec((K, tn), lambda i, j: (0, j))],
        out_specs=pl.BlockSpec((tm, tn), lambda i, j: (i, j)),
        out_shape=jax.ShapeDtypeStruct((T, N), out_dtype),
        compiler_params=_params("parallel", "arbitrary"),
        name="proj",
    )(x, w)


def _glu_proj_kernel(x_ref, wv_ref, wg_ref, bv_ref, bg_ref, o_ref):
    x = x_ref[...].astype(BF16)
    val = _dot(x, wv_ref[...]) + bv_ref[...]
    gate = _dot(x, wg_ref[...]) + bg_ref[...]
    o_ref[...] = (val * jax.nn.sigmoid(gate)).astype(o_ref.dtype)


def _glu_proj(x, w, b, out_dtype, tm=1024, tn=512):
    T, K = x.shape
    N = w.shape[1] // 2
    tm = min(tm, T)
    nj = N // tn
    b2 = b.reshape(1, 2 * N)
    return pl.pallas_call(
        _glu_proj_kernel,
        grid=(T // tm, nj),
        in_specs=[pl.BlockSpec((tm, K), lambda i, j: (i, 0)),
                  pl.BlockSpec((K, tn), lambda i, j: (0, j)),
                  pl.BlockSpec((K, tn), lambda i, j: (0, nj + j)),
                  pl.BlockSpec((1, tn), lambda i, j: (0, j)),
                  pl.BlockSpec((1, tn), lambda i, j: (0, nj + j))],
        out_specs=pl.BlockSpec((tm, tn), lambda i, j: (i, j)),
        out_shape=jax.ShapeDtypeStruct((T, N), out_dtype),
        compiler_params=_params("parallel", "arbitrary"),
        name="glu_proj",
    )(x, w, w, b2, b2)


def _out_ln_kernel(a_ref, w_ref, b_ref, res_ref, g_ref, beta_ref, o_ref, obf_ref):
    mix = _dot(a_ref[...], w_ref[...]) + b_ref[...]
    y = _layer_norm(ALPHA * res_ref[...] + mix, g_ref[...], beta_ref[...])
    o_ref[...] = y
    obf_ref[...] = y.astype(BF16)


def _out_ln_route_kernel(a_ref, w_ref, b_ref, res_ref, g_ref, beta_ref, wr_ref, br_ref,
                         o_ref, obf_ref, route_ref):
    mix = _dot(a_ref[...], w_ref[...]) + b_ref[...]
    y = _layer_norm(ALPHA * res_ref[...] + mix, g_ref[...], beta_ref[...])
    o_ref[...] = y
    obf_ref[...] = y.astype(BF16)
    l1 = jnp.sum(y * wr_ref[0:1, :], axis=-1, keepdims=True) + br_ref[0]
    i1 = jnp.zeros_like(l1)
    l2 = jnp.full_like(l1, -jnp.inf)
    i2 = jnp.zeros_like(l1)
    for e in range(1, N_EXPERTS):
        v = jnp.sum(y * wr_ref[e:e + 1, :], axis=-1, keepdims=True) + br_ref[e]
        gt1 = v > l1
        gt2 = v > l2
        l2 = jnp.where(gt1, l1, jnp.where(gt2, v, l2))
        i2 = jnp.where(gt1, i1, jnp.where(gt2, float(e), i2))
        l1 = jnp.where(gt1, v, l1)
        i1 = jnp.where(gt1, float(e), i1)
    e2 = jnp.exp(l2 - l1)
    w1 = 1.0 / (1.0 + e2)
    w2 = e2 / (1.0 + e2)
    lane = lax.broadcasted_iota(jnp.int32, route_ref.shape, 1)
    route_ref[...] = jnp.where(lane == 0, i1, jnp.where(lane == 1, i2, jnp.where(
        lane == 2, w1, jnp.where(lane == 3, w2, 0.0))))


def _out_ln(a, w, b, res, g, beta, router=None, tm=512):
    T, K = a.shape
    D = w.shape[1]
    row = lambda i: (i, 0)
    fixed = lambda i: (0, 0)
    in_specs = [pl.BlockSpec((tm, K), row), pl.BlockSpec((K, D), fixed), pl.BlockSpec((1, D), fixed),
                pl.BlockSpec((tm, D), row), pl.BlockSpec((1, D), fixed), pl.BlockSpec((1, D), fixed)]
    out_specs = [pl.BlockSpec((tm, D), row), pl.BlockSpec((tm, D), row)]
    out_shape = [jax.ShapeDtypeStruct((T, D), F32), jax.ShapeDtypeStruct((T, D), BF16)]
    args = [a, w, b.reshape(1, D), res, g.reshape(1, D), beta.reshape(1, D)]
    if router is None:
        body = _out_ln_kernel
    else:
        body = _out_ln_route_kernel
        w_r, b_r = router
        in_specs += [pl.BlockSpec((N_EXPERTS, D), fixed), pl.BlockSpec(memory_space=pltpu.SMEM)]
        out_specs.append(pl.BlockSpec((tm, LANES), row))
        out_shape.append(jax.ShapeDtypeStruct((T, LANES), F32))
        args += [w_r.T, b_r]
    return pl.pallas_call(
        body, grid=(T // tm,), in_specs=in_specs, out_specs=out_specs, out_shape=out_shape,
        compiler_params=_params("parallel"), name="out_ln",
    )(*args)


def _ln_ple(x1, ff, g, beta, wpg, p, wp):
    y = _layer_norm(ALPHA * x1 + ff, g, beta)
    gate = jax.nn.sigmoid(_dot(y.astype(BF16), wpg))
    return y + gate * _dot(p.astype(BF16), wp)


def _ffn_tail_kernel(x_ref, xbf_ref, wgu_ref, wd_ref, g_ref, beta_ref, wpg_ref, p_ref, wp_ref,
                     o_ref, obf_ref, *, tf):
    xb = xbf_ref[...]
    ff = None
    for c in range(D_FF // tf):
        gate = _dot(xb, wgu_ref[:, c * tf:(c + 1) * tf])
        up = _dot(xb, wgu_ref[:, D_FF + c * tf:D_FF + (c + 1) * tf])
        h = (jax.nn.silu(gate) * up).astype(BF16)
        part = _dot(h, wd_ref[c * tf:(c + 1) * tf, :])
        ff = part if ff is None else ff + part
    out = _ln_ple(x_ref[...], ff, g_ref[...], beta_ref[...], wpg_ref[...], p_ref[...], wp_ref[...])
    o_ref[...] = out
    obf_ref[...] = out.astype(BF16)


def _resident(shape):
    return pl.BlockSpec(shape, lambda i: (0,) * len(shape), pipeline_mode=pl.Buffered(1))


def _ffn_tail(x1, x1_bf, w_gu, w_down, g, beta, wpg, p, wp, tm=512, tf=256):
    T, D = x1.shape
    row = lambda i: (i, 0)
    return pl.pallas_call(
        functools.partial(_ffn_tail_kernel, tf=tf),
        grid=(T // tm,),
        in_specs=[pl.BlockSpec((tm, D), row), pl.BlockSpec((tm, D), row),
                  _resident((D, 2 * D_FF)), _resident((D_FF, D)),
                  _resident((1, D)), _resident((1, D)), _resident((D, D)),
                  pl.BlockSpec((tm, PLE_DIM), row), _resident((PLE_DIM, D))],
        out_specs=[pl.BlockSpec((tm, D), row), pl.BlockSpec((tm, D), row)],
        out_shape=[jax.ShapeDtypeStruct((T, D), F32), jax.ShapeDtypeStruct((T, D), BF16)],
        compiler_params=_params("parallel"),
        name="ffn_tail",
    )(x1, x1_bf, w_gu, w_down, g.reshape(1, D), beta.reshape(1, D), wpg, p, wp)


def _moe_tail_kernel(x_ref, y0_ref, y1_ref, route_ref, g_ref, beta_ref, wpg_ref, p_ref, wp_ref, o_ref, obf_ref):
    route = route_ref[...]
    lane = lax.broadcasted_iota(jnp.int32, route.shape, 1)
    ff = (y0_ref[...] * _lane_pick(route, lane, TOP_K) + y1_ref[...] * _lane_pick(route, lane, TOP_K + 1))
    out = _ln_ple(x_ref[...], ff, g_ref[...], beta_ref[...], wpg_ref[...], p_ref[...], wp_ref[...])
    o_ref[...] = out
    obf_ref[...] = out.astype(BF16)


def _moe_tail(x1, y0, y1, route, g, beta, wpg, p, wp, tm=512):
    T, D = x1.shape
    row = lambda i: (i, 0)
    return pl.pallas_call(
        _moe_tail_kernel,
        grid=(T // tm,),
        in_specs=[pl.BlockSpec((tm, D), row), pl.BlockSpec((tm, D), row), pl.BlockSpec((tm, D), row),
                  pl.BlockSpec((tm, LANES), row), _resident((1, D)), _resident((1, D)), _resident((D, D)),
                  pl.BlockSpec((tm, PLE_DIM), row), _resident((PLE_DIM, D))],
        out_specs=[pl.BlockSpec((tm, D), row), pl.BlockSpec((tm, D), row)],
        out_shape=[jax.ShapeDtypeStruct((T, D), F32), jax.ShapeDtypeStruct((T, D), BF16)],
        compiler_params=_params("parallel"),
        name="moe_tail",
    )(x1, y0, y1, route, g.reshape(1, D), beta.reshape(1, D), wpg, p, wp)


def _fox_decay_kernel(f_ref, bf_ref, c_ref):
    x = f_ref[...] + bf_ref[...]
    c_ref[...] = _scan_rows(None, -_softplus(-x), x.shape[0])


def _fox_decay(f_logit, b_f, batch, seq):
    return pl.pallas_call(
        _fox_decay_kernel,
        grid=(batch,),
        in_specs=[pl.BlockSpec((seq, LANES), lambda b: (b, 0)), pl.BlockSpec((1, LANES), lambda b: (0, 0))],
        out_specs=pl.BlockSpec((seq, LANES), lambda b: (b, 0)),
        out_shape=jax.ShapeDtypeStruct((batch * seq, LANES), F32),
        compiler_params=_params("parallel"),
        name="fox_decay",
    )(f_logit, b_f)


FOX_SUB = 128


def _fox_attn_kernel(q_ref, k_ref, v_ref, c_ref, o_ref, ckb_ref, vt_ref, *, tq):
    g = pl.program_id(1)
    qi = pl.program_id(2)
    S = k_ref.shape[0]
    ts = min(FOX_SUB, tq)
    nsub = tq // ts

    @pl.when(qi == 0)
    def _():
        c = c_ref[...]
        lane_s = lax.broadcasted_iota(jnp.int32, (S, LANES), 1)
        for hh in range(2):
            ckb_ref[hh] = jnp.broadcast_to(_lane_pick(c, lane_s, 2 * g + hh), (S, LANES))
        vt_ref[...] = v_ref[...].astype(F32).T.astype(BF16)

    lane = lax.broadcasted_iota(jnp.int32, (ts, LANES), 1)
    chains = [(i, hh) for i in range(nsub) for hh in range(2)]
    qs = []
    for i, hh in chains:
        q = q_ref[i * ts:(i + 1) * ts, :] * jnp.asarray(FOX_HEAD_DIM ** -0.5, BF16)
        qs.append(jnp.where((lane >= FOX_HEAD_DIM) == (hh == 1), q, jnp.zeros_like(q)))

    def block(j, carry, diagonal):
        start = pl.multiple_of(j * tq, tq)
        width = [(i + 1) * ts if diagonal else tq for i in range(nsub)]
        qk = [_dot_nt(k_ref[pl.ds(start, width[i]), :], qs[c]) for c, (i, hh) in enumerate(chains)]
        stats = []
        for c, (i, hh) in enumerate(chains):
            m, l, _ = carry[c]
            s = qk[c] - ckb_ref[hh, pl.ds(start, width[i]), :]
            if diagonal:
                key = lax.broadcasted_iota(jnp.int32, s.shape, 0)
                qry = lax.broadcasted_iota(jnp.int32, s.shape, 1)
                s = jnp.where(key <= qry + i * ts, s, NEG_BIG)
            m_new = jnp.maximum(m, jnp.max(s, axis=0, keepdims=True))
            scale = jnp.exp(m - m_new)
            pr = jnp.exp(s - m_new)
            stats.append((m_new, scale * l + jnp.sum(pr, axis=0, keepdims=True), scale, pr.astype(BF16)))
        pv = [_dot(vt_ref[:, pl.ds(start, width[i])], stats[c][3]) for c, (i, hh) in enumerate(chains)]
        return tuple((stats[c][0], stats[c][1], stats[c][2] * carry[c][2] + pv[c]) for c in range(len(chains)))

    init = tuple((jnp.full((1, ts), NEG_BIG, F32), jnp.zeros((1, ts), F32), jnp.zeros((LANES, ts), F32))
                 for _ in chains)
    carry = lax.fori_loop(0, qi, lambda j, c: block(j, c, False), init)
    final = block(qi, carry, True)
    dim = lax.broadcasted_iota(jnp.int32, (LANES, ts), 0)
    for i in range(nsub):
        (_, l0, a0), (_, l1, a1) = final[2 * i], final[2 * i + 1]
        out_t = jnp.where(dim < FOX_HEAD_DIM, a0 / l0, a1 / l1)
        o_ref[i * ts:(i + 1) * ts, :] = out_t.T.astype(o_ref.dtype)


def _fox_attn(qkv, c, batch, seq, tq=512):
    tq = min(tq, seq)
    nq = seq // tq
    pairs = FOX_HEADS // 2
    return pl.pallas_call(
        functools.partial(_fox_attn_kernel, tq=tq),
        grid=(batch, pairs, nq),
        in_specs=[pl.BlockSpec((tq, LANES), lambda b, g, i: (b * nq + i, g)),
                  pl.BlockSpec((seq, LANES), lambda b, g, i: (b, pairs + g)),
                  pl.BlockSpec((seq, LANES), lambda b, g, i: (b, 2 * pairs + g)),
                  pl.BlockSpec((seq, LANES), lambda b, g, i: (b, 0))],
        out_specs=pl.BlockSpec((tq, LANES), lambda b, g, i: (b * nq + i, g)),
        out_shape=jax.ShapeDtypeStruct((batch * seq, D_MODEL), BF16),
        scratch_shapes=[pltpu.VMEM((2, seq, LANES), F32), pltpu.VMEM((LANES, seq), BF16)],
        compiler_params=_params("parallel", "parallel", "arbitrary"),
        name="fox_attn",
    )(qkv, qkv, qkv, c)


def _pad_lanes(w):
    return jnp.zeros((w.shape[0], LANES), w.dtype).at[:, :w.shape[1]].set(w)


def _fox_mixer(x, x_bf, batch, seq, w_in, b_f, w_out, ln_g, ln_b, router=None):
    D = D_MODEL
    qkv = _proj(x_bf, w_in[:, :3 * D].astype(BF16), BF16)
    f_logit = _proj(x_bf, _pad_lanes(w_in[:, 3 * D:]).astype(BF16), F32, tn=LANES)
    c = _fox_decay(f_logit, _pad_lanes(b_f.reshape(1, -1)), batch, seq)
    o = _fox_attn(qkv, c, batch, seq)
    return _out_ln(o, w_out.astype(BF16), jnp.zeros((D,), F32), x, ln_g, ln_b, router)


PAD_ROWS = 8


def _causal_conv4(src, cw_ref, pad_ref):
    S = src.shape[0]
    pad_ref[0:PAD_ROWS, :] = jnp.zeros((PAD_ROWS, src.shape[1]), F32)
    pad_ref[PAD_ROWS:PAD_ROWS + S, :] = src
    out = None
    for k in range(4):
        off = PAD_ROWS - 3 + k
        term = cw_ref[k:k + 1, :] * pad_ref[off:off + S, :]
        out = term if out is None else out + term
    return out


def _lru_core_kernel(gate_ref, rec_ref, cw_ref, cb_ref, wa_ref, ba_ref, wx_ref, bx_ref, lam_ref, y_ref, pad_ref):
    S = rec_ref.shape[0]
    u = _causal_conv4(rec_ref[...].astype(F32), cw_ref, pad_ref) + cb_ref[...]
    ub = u.astype(BF16)
    r = jax.nn.sigmoid(_dot(ub, wa_ref[...]) + ba_ref[...])
    i = jax.nn.sigmoid(_dot(ub, wx_ref[...]) + bx_ref[...])
    log_a = -LRU_C * r * _softplus(-lam_ref[...])
    a = jnp.exp(log_a)
    th = jnp.tanh(log_a)
    b = jnp.sqrt(-2.0 * th / (1.0 - th)) * (i * u)
    h = _scan_rows(a, b, S)
    y_ref[...] = (jax.nn.gelu(gate_ref[...].astype(F32)) * h).astype(y_ref.dtype)


def _lru_core(gr, conv_w, conv_b, w_a, b_a, w_x, b_x, lam, batch, seq):
    nb = LRU_BLOCKS
    vec = lambda v: v.reshape(1, LRU_WIDTH)
    lane_blk = pl.BlockSpec((1, LANES), lambda b, n: (0, n))
    mat_blk = pl.BlockSpec((None, LANES, LANES), lambda b, n: (n, 0, 0))
    return pl.pallas_call(
        _lru_core_kernel,
        grid=(batch, nb),
        in_specs=[pl.BlockSpec((seq, LANES), lambda b, n: (b, n)),
                  pl.BlockSpec((seq, LANES), lambda b, n: (b, nb + n)),
                  pl.BlockSpec((LRU_CONV, LANES), lambda b, n: (0, n)),
                  lane_blk, mat_blk, lane_blk, mat_blk, lane_blk, lane_blk],
        out_specs=pl.BlockSpec((seq, LANES), lambda b, n: (b, n)),
        out_shape=jax.ShapeDtypeStruct((batch * seq, LRU_WIDTH), BF16),
        scratch_shapes=[pltpu.VMEM((seq + PAD_ROWS, LANES), F32)],
        compiler_params=_params("parallel", "parallel"),
        name="lru_core",
    )(gr, gr, conv_w, vec(conv_b), w_a.astype(BF16), vec(b_a), w_x.astype(BF16), vec(b_x), vec(lam))


def _lru_mixer(x, x_bf, batch, seq, w_in, conv_w, conv_b, w_a, b_a, w_x, b_x, lam, w_out, ln_g, ln_b,
               router=None):
    gr = _proj(x_bf, w_in.astype(BF16), BF16, tn=LRU_WIDTH)
    y = _lru_core(gr, conv_w, conv_b, w_a, b_a, w_x, b_x, lam, batch, seq)
    return _out_ln(y, w_out.astype(BF16), jnp.zeros((D_MODEL,), F32), x, ln_g, ln_b, router)


CONF_HALO = 32


def _conf_tail_kernel(h_ref, halo_ref, dw_ref, dwb_ref, cg_ref, cb_ref, w_ref, b_ref, res_ref, g_ref, beta_ref,
                      o_ref, obf_ref, buf_ref, *, tt):
    first = pl.program_id(1) == 0
    halo = halo_ref[...].astype(F32)
    buf_ref[0:CONF_HALO, :] = jnp.where(first, 0.0, halo)
    buf_ref[CONF_HALO:CONF_HALO + tt, :] = h_ref[...].astype(F32)
    first = CONF_HALO - (CONF_KERNEL - 1)
    conv = None
    for s in range(SUBLANES):
        offs = [o for o in range(first, first + CONF_KERNEL) if o % SUBLANES == s]
        span = offs[-1] - s + tt
        slab = buf_ref[s:s + span, :]
        part = None
        for o in offs:
            term = dw_ref[o - first:o - first + 1, :] * slab[o - s:o - s + tt]
            part = term if part is None else part + term
        conv = part if conv is None else conv + part
    hs = jax.nn.silu(_layer_norm(conv + dwb_ref[...], cg_ref[...], cb_ref[...]))
    mix = _dot(hs.astype(BF16), w_ref[...]) + b_ref[...]
    y = _layer_norm(ALPHA * res_ref[...] + mix, g_ref[...], beta_ref[...])
    o_ref[...] = y
    obf_ref[...] = y.astype(BF16)


def _conf_tail(h, dw_w, dw_b, cg, cb, w_out, b_out, res, ln_g, ln_b, batch, seq, tt=256):
    D = D_MODEL
    nt = seq // tt
    per = tt // CONF_HALO
    row = lambda b, i: (b * nt + i, 0)
    fixed = lambda b, i: (0, 0)
    vec = lambda v: v.reshape(1, D)
    return pl.pallas_call(
        functools.partial(_conf_tail_kernel, tt=tt),
        grid=(batch, nt),
        in_specs=[pl.BlockSpec((tt, D), row),
                  pl.BlockSpec((CONF_HALO, D), lambda b, i: (jnp.maximum((b * nt + i) * per - 1, 0), 0)),
                  pl.BlockSpec((CONF_KERNEL, D), fixed), pl.BlockSpec((1, D), fixed),
                  pl.BlockSpec((1, D), fixed), pl.BlockSpec((1, D), fixed),
                  pl.BlockSpec((D, D), fixed), pl.BlockSpec((1, D), fixed),
                  pl.BlockSpec((tt, D), row), pl.BlockSpec((1, D), fixed), pl.BlockSpec((1, D), fixed)],
        out_specs=[pl.BlockSpec((tt, D), row), pl.BlockSpec((tt, D), row)],
        out_shape=[jax.ShapeDtypeStruct((batch * seq, D), F32), jax.ShapeDtypeStruct((batch * seq, D), BF16)],
        scratch_shapes=[pltpu.VMEM((CONF_HALO + tt, D), F32)],
        compiler_params=_params("parallel", "arbitrary"),
        name="conf_tail",
    )(h, h, dw_w, vec(dw_b), vec(cg), vec(cb), w_out, vec(b_out), res, vec(ln_g), vec(ln_b))


def _conf_mixer(x, x_bf, batch, seq, w_in, b_in, dw_w, dw_b, cg, cb, w_out, b_out, ln_g, ln_b):
    h = _glu_proj(x_bf, w_in.astype(BF16), b_in, BF16)
    return _conf_tail(h, dw_w, dw_b, cg, cb, w_out.astype(BF16), b_out, x, ln_g, ln_b, batch, seq)


def _gdn_core_kernel(q_ref, k_ref, v_ref, z_ref, ba_ref, cwq_ref, cwk_ref, cwv_ref, alog_ref, dtb_ref, ng_ref,
                     pm_ref, o_ref,
                     pad_ref, qs_ref, ks_ref, kb_ref, vb_ref, kbe_ref, gcs_ref,
                     u_ref, w_ref, kend_ref, qg_ref, attn_ref, egl_ref, *, hb, tile):
    S = q_ref.shape[0]
    C = tile
    Dh = GDN_HEAD_DIM
    n_chunks = S // C
    grp = pl.program_id(1)
    lane = lax.broadcasted_iota(jnp.int32, (S, LANES), 1)

    ba = ba_ref[...]
    beta_all = jax.nn.sigmoid(ba)
    g_all = -jnp.exp(alog_ref[...]) * _softplus(ba + dtb_ref[...])
    gc_all = _scan_rows(None, g_all, C)

    def l2n(t):
        return t * lax.rsqrt(jnp.sum(t * t, axis=-1, keepdims=True) + 1e-6)

    for hh in range(hb):
        head = grp * hb + hh
        hs = slice(hh * Dh, (hh + 1) * Dh)
        beta = _lane_pick(beta_all, lane, head)
        gc = _lane_pick(gc_all, lane, GDN_HEADS + head)
        q = jax.nn.silu(_causal_conv4(q_ref[:, hs].astype(F32), cwq_ref.at[:, hs], pad_ref))
        k = jax.nn.silu(_causal_conv4(k_ref[:, hs].astype(F32), cwk_ref.at[:, hs], pad_ref))
        v = jax.nn.silu(_causal_conv4(v_ref[:, hs].astype(F32), cwv_ref.at[:, hs], pad_ref))
        q = l2n(q) * (Dh ** -0.5)
        k = l2n(k)
        eg = jnp.exp(gc)
        kb = k * beta
        qs_ref[hh] = q.astype(BF16)
        ks_ref[hh] = k.astype(BF16)
        kb_ref[hh] = kb.astype(BF16)
        vb_ref[hh] = (v * beta).astype(BF16)
        kbe_ref[hh] = (kb * eg).astype(BF16)
        qg_ref[hh] = (q * eg).astype(BF16)
        gcs_ref[hh] = jnp.broadcast_to(gc, (S, LANES))

    row = lax.broadcasted_iota(jnp.int32, (C, C), 0)
    col = lax.broadcasted_iota(jnp.int32, (C, C), 1)
    lane_c = lax.broadcasted_iota(jnp.int32, (C, LANES), 1)
    n_levels = pm_ref.shape[0]
    chunks_per_step = 2 if n_chunks % 2 == 0 else 1

    def split_features(gcl):
        hi = gcl.astype(BF16).astype(F32)
        r1 = gcl - hi
        mid = r1.astype(BF16).astype(F32)
        lo = r1 - mid
        fa = jnp.where(lane_c == 0, hi, jnp.where(lane_c == 1, mid, jnp.where(
            lane_c == 2, lo, jnp.where(lane_c < 6, 1.0, 0.0))))
        fb = jnp.where(lane_c < 3, 1.0, jnp.where(lane_c == 3, -hi, jnp.where(
            lane_c == 4, -mid, jnp.where(lane_c == 5, -lo, 0.0))))
        return fa.astype(BF16), fb.astype(BF16)

    def prep(n, _):
        chains = [(hh, n * chunks_per_step + cc) for cc in range(chunks_per_step) for hh in range(hb)]
        sls = [pl.ds(pl.multiple_of(cn * C, C), C) for _, cn in chains]
        heads = [hh for hh, _ in chains]
        ch = range(len(chains))
        kc = [ks_ref[heads[i], sls[i], :] for i in ch]
        gcl = [gcs_ref[heads[i], sls[i], :] for i in ch]
        feats = [split_features(gcl[i]) for i in ch]
        gdiff = [_dot_nt(*feats[i]) for i in ch]
        kk = [_dot_nt(kb_ref[heads[i], sls[i], :], kc[i]) for i in ch]
        qk = [_dot_nt(qs_ref[heads[i], sls[i], :], kc[i]) for i in ch]
        tril = row >= col
        decay = [jnp.where(tril, jnp.exp(jnp.where(tril, gdiff[i], 0.0)), 0.0) for i in ch]
        lower = [(kk[i] * decay[i]).astype(BF16) for i in ch]
        for i in ch:
            attn_ref[heads[i], sls[i], :] = (qk[i] * decay[i]).astype(BF16)
        eye = jnp.where(row == col, 1.0, 0.0)
        inv = [eye - (lower[i] * pm_ref[0]).astype(F32) for i in ch]
        for lvl in range(1, n_levels):
            inv_b = [inv[i].astype(BF16) for i in ch]
            part = [_dot(lower[i] * pm_ref[lvl], inv_b[i]).astype(BF16) for i in ch]
            inv = [inv[i] - _dot(inv_b[i], part[i]) for i in ch]
        inv_b = [inv[i].astype(BF16) for i in ch]
        u = [_dot(inv_b[i], vb_ref[heads[i], sls[i], :]) for i in ch]
        w = [_dot(inv_b[i], kbe_ref[heads[i], sls[i], :]) for i in ch]
        for i in ch:
            hh, cn = chains[i]
            u_ref[hh, sls[i], :] = u[i]
            w_ref[hh, sls[i], :] = w[i].astype(BF16)
            gl = gcl[i][C - 1:C, :]
            kend_ref[hh, sls[i], :] = (kc[i].astype(F32) * jnp.exp(gl - gcl[i])).astype(BF16)
            egl_ref[hh, pl.ds(pl.multiple_of(cn * 8, 8), 8), :] = jnp.broadcast_to(jnp.exp(gl), (8, LANES))
        return 0

    lax.fori_loop(0, n_chunks // chunks_per_step, prep, 0)

    ng = ng_ref[...]

    def step(n, states):
        sl = pl.ds(pl.multiple_of(n * C, C), C)
        hr = range(hb)
        st_b = [states[hh].astype(BF16) for hh in hr]
        w_st = [_dot(w_ref[hh, sl, :], st_b[hh]) for hh in hr]
        q_st = [_dot(qg_ref[hh, sl, :], st_b[hh]) for hh in hr]
        v_new_b = [(u_ref[hh, sl, :] - w_st[hh]).astype(BF16) for hh in hr]
        intra = [_dot(attn_ref[hh, sl, :], v_new_b[hh]) for hh in hr]
        grow = [_dot_tn(kend_ref[hh, sl, :], v_new_b[hh]) for hh in hr]
        new = []
        for hh in hr:
            egl = egl_ref[hh, pl.ds(pl.multiple_of(n * 8, 8), 8), :][0:1, 0:1]
            new.append(states[hh] * egl + grow[hh])
            o = q_st[hh] + intra[hh]
            o = o * lax.rsqrt(jnp.mean(o * o, axis=-1, keepdims=True) + RMS_EPS) * ng
            zc = z_ref[sl, hh * Dh:(hh + 1) * Dh].astype(F32)
            o_ref[sl, hh * Dh:(hh + 1) * Dh] = (o * jax.nn.silu(zc)).astype(o_ref.dtype)
        return tuple(new)

    lax.fori_loop(0, n_chunks, step, tuple(jnp.zeros((Dh, Dh), F32) for _ in range(hb)))


def _pair_masks(tile):
    r = jnp.arange(tile)[:, None]
    c = jnp.arange(tile)[None, :]
    levels = []
    m = 1
    while m < tile:
        levels.append(((r // m) == (c // m) + 1) & ((r // (2 * m)) == (c // (2 * m))))
        m *= 2
    return jnp.stack(levels).astype(BF16)


def _gdn_core(qkvz, ba, conv_w, a_log, dt_bias, norm_g, batch, seq, hb=2, tile=GDN_TILE):
    H, Dh = GDN_HEADS, GDN_HEAD_DIM
    W = hb * Dh
    ng_ = H // hb
    tile = min(tile, seq)
    masks = _pair_masks(tile)
    blk = lambda part: pl.BlockSpec((seq, W), lambda b, g: (b, part * ng_ + g))
    cw = lambda part: pl.BlockSpec((GDN_CONV, W), lambda b, g: (0, part * ng_ + g))
    fixed = pl.BlockSpec((1, LANES), lambda b, g: (0, 0))
    pad = lambda v: jnp.zeros((1, LANES), F32).at[0, H:2 * H].set(v)
    return pl.pallas_call(
        functools.partial(_gdn_core_kernel, hb=hb, tile=tile),
        grid=(batch, ng_),
        in_specs=[blk(0), blk(1), blk(2), blk(3), pl.BlockSpec((seq, LANES), lambda b, g: (b, 0)),
                  cw(0), cw(1), cw(2), fixed, fixed, fixed,
                  pl.BlockSpec(masks.shape, lambda b, g: (0, 0, 0))],
        out_specs=pl.BlockSpec((seq, W), lambda b, g: (b, g)),
        out_shape=jax.ShapeDtypeStruct((batch * seq, H * Dh), BF16),
        scratch_shapes=[pltpu.VMEM((seq + PAD_ROWS, Dh), F32)]
                       + [pltpu.VMEM((hb, seq, Dh), BF16)] * 5
                       + [pltpu.VMEM((hb, seq, LANES), F32),
                          pltpu.VMEM((hb, seq, Dh), F32),
                          pltpu.VMEM((hb, seq, Dh), BF16),
                          pltpu.VMEM((hb, seq, Dh), BF16),
                          pltpu.VMEM((hb, seq, Dh), BF16),
                          pltpu.VMEM((hb, seq, tile), BF16),
                          pltpu.VMEM((hb, seq // tile * 8, LANES), F32)],
        compiler_params=_params("parallel", "parallel"),
        name="gdn_core",
    )(qkvz, qkvz, qkvz, qkvz, ba, conv_w, conv_w, conv_w, pad(a_log), pad(dt_bias), norm_g.reshape(1, Dh), masks)


def _gdn_mixer(x, x_bf, batch, seq, w_in, conv_w, a_log, dt_bias, norm_g, w_out, ln_g, ln_b, router=None):
    HD = GDN_HEADS * GDN_HEAD_DIM
    qkvz = _proj(x_bf, w_in[:, :4 * HD].astype(BF16), BF16)
    ba = _proj(x_bf, _pad_lanes(w_in[:, 4 * HD:]).astype(BF16), F32, tn=LANES)
    o = _gdn_core(qkvz, ba, conv_w, a_log, dt_bias, norm_g, batch, seq)
    return _out_ln(o, w_out.astype(BF16), jnp.zeros((D_MODEL,), F32), x, ln_g, ln_b, router)


MOE_TILE = 1024


def _moe_expert_kernel(te_ref, tr_ref, x_ref, wg_ref, wu_ref, wd_ref, o_ref, xb_ref):
    i = pl.program_id(0)
    j = pl.program_id(1)
    rows = tr_ref[i]
    half = MOE_TILE // 2
    halves = [pl.ds(0, half), pl.ds(half, half)]

    @pl.when(j == 0)
    def _():
        xb_ref[...] = x_ref[...].astype(BF16)
        o_ref[...] = jnp.zeros_like(o_ref)

    def swiglu_chunk(parts):
        wg = wg_ref[...].astype(BF16)
        wu = wu_ref[...].astype(BF16)
        wd = wd_ref[...].astype(BF16)
        gate = [_dot(xb_ref[h, :], wg) for h in parts]
        up = [_dot(xb_ref[h, :], wu) for h in parts]
        hid = [(jax.nn.silu(gate[k]) * up[k]).astype(BF16) for k in range(len(parts))]
        for k, h in enumerate(parts):
            o_ref[h, :] += _dot(hid[k], wd)

    @pl.when(rows > half)
    def _():
        swiglu_chunk(halves)

    @pl.when((rows > 0) & (rows <= half))
    def _():
        swiglu_chunk(halves[:1])


def _moe_experts(xs, tile_e, tile_rows, w_gu, w_down, layer, tf=512):
    n_rows, D = xs.shape
    n_tiles = n_rows // MOE_TILE
    nc = D_EXPERT // tf

    def chunk(j, tr, i):
        return jnp.where(tr[i] > 0, j, nc - 1)

    grid_spec = pltpu.PrefetchScalarGridSpec(
        num_scalar_prefetch=2,
        grid=(n_tiles, nc),
        in_specs=[pl.BlockSpec((MOE_TILE, D), lambda i, j, te, tv: (i, 0)),
                  pl.BlockSpec((None, None, D, tf), lambda i, j, te, tv: (layer, te[i], 0, chunk(j, tv, i))),
                  pl.BlockSpec((None, None, D, tf), lambda i, j, te, tv: (layer, te[i], 0, nc + chunk(j, tv, i))),
                  pl.BlockSpec((None, None, tf, D), lambda i, j, te, tv: (layer, te[i], chunk(j, tv, i), 0))],
        out_specs=pl.BlockSpec((MOE_TILE, D), lambda i, j, te, tv: (i, 0)),
        scratch_shapes=[pltpu.VMEM((MOE_TILE, D), BF16)],
    )
    return pl.pallas_call(
        _moe_expert_kernel,
        grid_spec=grid_spec,
        out_shape=jax.ShapeDtypeStruct((n_rows, D), F32),
        compiler_params=_params("arbitrary", "arbitrary"),
        name="moe_experts",
    )(tile_e, tile_rows, xs, w_gu, w_gu, w_down)


def _moe_ffn(x1, route, w_gu, w_down, layer, g, beta, wpg, p, wp):
    T, D = x1.shape
    n_pairs = T * TOP_K
    n_tiles = n_pairs // MOE_TILE + N_EXPERTS
    n_rows = n_tiles * MOE_TILE
    flat_e = route[:, :TOP_K].astype(jnp.int32).reshape(-1)
    onehot = (flat_e[:, None] == jnp.arange(N_EXPERTS, dtype=jnp.int32)[None, :]).astype(jnp.int32)
    csum = jnp.cumsum(onehot, axis=0)
    counts = csum[-1]
    rank = jnp.sum((csum - 1) * onehot, axis=1)
    padded = ((counts + MOE_TILE - 1) // MOE_TILE) * MOE_TILE
    pad_end = jnp.cumsum(padded)
    pad_start = pad_end - padded
    dest = pad_start[flat_e] + rank
    row_tok = jnp.zeros((n_rows,), jnp.int32).at[dest].set(
        jnp.arange(n_pairs, dtype=jnp.int32) // TOP_K, unique_indices=True, mode="promise_in_bounds")
    tile_start = jnp.arange(n_tiles, dtype=jnp.int32) * MOE_TILE
    tile_valid = tile_start < pad_end[-1]
    tile_e = jnp.minimum(jnp.searchsorted(pad_end, tile_start, side="right"), N_EXPERTS - 1).astype(jnp.int32)
    tile_rows = jnp.clip((pad_start + counts)[tile_e] - tile_start, 0, MOE_TILE)
    tile_rows = jnp.where(tile_valid, tile_rows, 0).astype(jnp.int32)
    tile_e = jnp.where(tile_valid, tile_e, jnp.max(jnp.where(tile_valid, tile_e, 0)))
    take = lambda table, idx: table.at[idx].get(mode="promise_in_bounds")
    y = _moe_experts(take(x1, row_tok), tile_e, tile_rows, w_gu, w_down, layer)
    dest2 = dest.reshape(T, TOP_K)
    return _moe_tail(x1, take(y, dest2[:, 0]), take(y, dest2[:, 1]), route, g, beta, wpg, p, wp)


def kernel(x, p, ln_mix_g, ln_mix_b, ln_ffn_g, ln_ffn_b, ple_w, ple_gate_w, fox_w_in, fox_b_f, fox_w_out, lru_w_in, lru_conv_w, lru_conv_b, lru_w_a, lru_b_a, lru_w_x, lru_b_x, lru_lambda, lru_w_out, cv_w_in, cv_b_in, cv_dw_w, cv_dw_b, cv_ln_g, cv_ln_b, cv_w_out, cv_b_out, gdn_w_in, gdn_conv_w, gdn_a_log, gdn_dt_bias, gdn_norm_g, gdn_w_out, ffn_w_gu, ffn_w_down, moe_w_router, moe_b_router, moe_w_gu, moe_w_down):
    B, S, D = x.shape
    groups = BATCH_GROUPS if B % BATCH_GROUPS == 0 else 1
    Bg = B // groups
    T = Bg * S
    xf = [x[gi * Bg:(gi + 1) * Bg].reshape(T, D) for gi in range(groups)]
    xb = list(xf)
    for i in range(DEPTH):
        m, j = i % 4, i // 4
        router = (moe_w_router[i // 2], moe_b_router[i // 2]) if i % 2 == 1 else None
        ln = (ln_mix_g[i], ln_mix_b[i])
        wpg = ple_gate_w[i].astype(BF16)
        wp = ple_w[i].astype(BF16)
        for gi in range(groups):
            args = (xf[gi], xb[gi], Bg, S)
            if m == 0:
                mixed = _fox_mixer(*args, fox_w_in[j], fox_b_f[j], fox_w_out[j], *ln, router)
            elif m == 1:
                mixed = _lru_mixer(*args, lru_w_in[j], lru_conv_w[j], lru_conv_b[j], lru_w_a[j], lru_b_a[j],
                                   lru_w_x[j], lru_b_x[j], lru_lambda[j], lru_w_out[j], *ln, router)
            elif m == 2:
                mixed = _conf_mixer(*args, cv_w_in[j], cv_b_in[j], cv_dw_w[j], cv_dw_b[j], cv_ln_g[j],
                                    cv_ln_b[j], cv_w_out[j], cv_b_out[j], *ln)
            else:
                mixed = _gdn_mixer(*args, gdn_w_in[j], gdn_conv_w[j], gdn_a_log[j], gdn_dt_bias[j],
                                   gdn_norm_g[j], gdn_w_out[j], *ln, router)
            tail = (ln_ffn_g[i], ln_ffn_b[i], wpg, p[i, gi * Bg:(gi + 1) * Bg].reshape(T, PLE_DIM), wp)
            if i % 2 == 0:
                x1, x1b = mixed
                xf[gi], xb[gi] = _ffn_tail(x1, x1b, ffn_w_gu[i // 2].astype(BF16),
                                           ffn_w_down[i // 2].astype(BF16), *tail)
            else:
                x1, x1b, route = mixed
                xf[gi], xb[gi] = _moe_ffn(x1, route, moe_w_gu, moe_w_down, i // 2, *tail)
    return jnp.concatenate([t.reshape(Bg, S, D) for t in xf], axis=0)
```
